```python
import jax, jax.numpy as jnp
from jax import lax
import numpy as np

D_MODEL = 1024
BATCH = 8
SEQ = 4096
DEPTH = 4

N_MIXERS = 2
N_POOL_LAYERS = (DEPTH + 1) // 2
N_MLA_LAYERS = DEPTH // 2

POOL_WINDOWS = (2, 4, 8, 16)
N_POOL_GROUPS = len(POOL_WINDOWS)
POOL_GROUP = D_MODEL // N_POOL_GROUPS

N_HEADS = D_MODEL // 128
QK_NOPE_DIM = 128
QK_ROPE_DIM = 64
V_HEAD_DIM = 128
Q_LORA_RANK = 3 * D_MODEL // 8
KV_LORA_RANK = D_MODEL // 8
QK_HEAD_DIM = QK_NOPE_DIM + QK_ROPE_DIM
ROPE_THETA = 10000.0
Q_BLOCK = 128

D_FF = ((8 * D_MODEL // 3 + 255) // 256) * 256

RMS_EPS = 1e-6

kernel_name = "hybrid_pool_mla_swiglu_trunk"


def rms_norm(x, g):
    xf = x.astype(jnp.float32)
    y = xf * lax.rsqrt(jnp.mean(xf * xf, axis=-1, keepdims=True) + RMS_EPS)
    return (y * g.astype(jnp.float32)).astype(x.dtype)


def pool_mixer(h, w, scale):
    B, S, D = h.shape
    hf = h.astype(jnp.float32).reshape(B, S, N_POOL_GROUPS, POOL_GROUP)
    cs = jnp.cumsum(hf, axis=1)
    n_avail = jnp.arange(1, S + 1, dtype=jnp.float32)
    outs = []
    for g, win in enumerate(POOL_WINDOWS):
        c = cs[:, :, g]
        lag = jnp.pad(c[:, :-win], ((0, 0), (win, 0), (0, 0)))
        mean = (c - lag) / jnp.minimum(n_avail, float(win))[:, None]
        outs.append(mean - hf[:, :, g])
    p = jnp.stack(outs, axis=2).astype(h.dtype)
    y = jnp.einsum('bsgc,gcd->bsgd', p, w).reshape(B, S, D)
    return y * scale


def rope_tables(positions):
    inv = 1.0 / (ROPE_THETA ** (jnp.arange(0, QK_ROPE_DIM, 2, dtype=jnp.float32) / QK_ROPE_DIM))
    ang = positions.astype(jnp.float32)[..., None] * inv
    return jnp.cos(ang), jnp.sin(ang)


def apply_rope(x, cos, sin):
    xf = x.astype(jnp.float32)
    half = QK_ROPE_DIM // 2
    x1, x2 = xf[..., :half], xf[..., half:]
    return jnp.concatenate([x1 * cos - x2 * sin, x2 * cos + x1 * sin], axis=-1).astype(x.dtype)


def causal_attention(q, k, v):
    B, S, H, Dq = q.shape
    nb = S // Q_BLOCK
    qb = q.reshape(B, nb, Q_BLOCK, H, Dq).transpose(1, 0, 2, 3, 4)
    key_pos = jnp.arange(S)
    sm_scale = Dq ** -0.5

    def one_block(args):
        blk, qi = args
        s = jnp.einsum('bqhd,bkhd->bhqk', qi, k).astype(jnp.float32) * sm_scale
        q_pos = blk * Q_BLOCK + jnp.arange(Q_BLOCK)
        mask = key_pos[None, :] <= q_pos[:, None]
        s = jnp.where(mask, s, -jnp.inf)
        p = jax.nn.softmax(s, axis=-1).astype(v.dtype)
        return jnp.einsum('bhqk,bkhd->bqhd', p, v)

    o = lax.map(one_block, (jnp.arange(nb), qb))
    return o.transpose(1, 0, 2, 3, 4).reshape(B, S, H, v.shape[-1])


def mla(h, cos, sin, w_down, q_norm, w_uq, kv_norm, w_ukv, w_o):
    B, S, _ = h.shape
    d = h @ w_down
    cq = rms_norm(d[..., :Q_LORA_RANK], q_norm)
    ckv = rms_norm(d[..., Q_LORA_RANK:Q_LORA_RANK + KV_LORA_RANK], kv_norm)
    k_rope = apply_rope(d[..., Q_LORA_RANK + KV_LORA_RANK:], cos, sin)
    q = (cq @ w_uq).reshape(B, S, N_HEADS, QK_HEAD_DIM)
    q_rope = apply_rope(q[..., QK_NOPE_DIM:], cos[:, :, None], sin[:, :, None])
    q = jnp.concatenate([q[..., :QK_NOPE_DIM], q_rope], axis=-1)
    kv = (ckv @ w_ukv).reshape(B, S, N_HEADS, QK_NOPE_DIM + V_HEAD_DIM)
    k = jnp.concatenate(
        [kv[..., :QK_NOPE_DIM],
         jnp.broadcast_to(k_rope[:, :, None, :], (B, S, N_HEADS, QK_ROPE_DIM))], axis=-1)
    v = kv[..., QK_NOPE_DIM:]
    o = causal_attention(q, k, v)
    return o.reshape(B, S, N_HEADS * V_HEAD_DIM) @ w_o


def swiglu(h, w_gate, w_up, w_down):
    return (jax.nn.silu(h @ w_gate) * (h @ w_up)) @ w_down


def setup_inputs(seed: int = 0) -> dict:
    key = jax.random.key(seed)
    ks = jax.random.split(key, 16)
    f32 = jnp.float32

    def nrm(k, shape, fan_in):
        return jax.random.normal(k, shape, f32) * (fan_in ** -0.5)

    def gain(k, shape):
        return 1.0 + 0.02 * jax.random.normal(k, shape, f32)

    x = jax.random.normal(ks[0], (BATCH, SEQ, D_MODEL), f32)
    positions = jnp.broadcast_to(jnp.arange(SEQ, dtype=jnp.int32)[None, :], (BATCH, SEQ))
    return {
        "x": x,
        "positions": positions,
        "norm_mix": gain(ks[1], (DEPTH, D_MODEL)),
        "norm_ffn": gain(ks[2], (DEPTH, D_MODEL)),
        "norm_final": gain(ks[3], (D_MODEL,)),
        "pool_w": nrm(ks[4], (N_POOL_LAYERS, N_POOL_GROUPS, POOL_GROUP, POOL_GROUP), POOL_GROUP),
        "pool_scale": gain(ks[5], (N_POOL_LAYERS, D_MODEL)),
        "mla_w_down": nrm(ks[6], (N_MLA_LAYERS, D_MODEL, Q_LORA_RANK + KV_LORA_RANK + QK_ROPE_DIM), D_MODEL),
        "mla_q_norm": gain(ks[7], (N_MLA_LAYERS, Q_LORA_RANK)),
        "mla_w_uq": nrm(ks[8], (N_MLA_LAYERS, Q_LORA_RANK, N_HEADS * QK_HEAD_DIM), Q_LORA_RANK),
        "mla_kv_norm": gain(ks[9], (N_MLA_LAYERS, KV_LORA_RANK)),
        "mla_w_ukv": nrm(ks[10], (N_MLA_LAYERS, KV_LORA_RANK, N_HEADS * (QK_NOPE_DIM + V_HEAD_DIM)), KV_LORA_RANK),
        "mla_w_o": nrm(ks[11], (N_MLA_LAYERS, N_HEADS * V_HEAD_DIM, D_MODEL), N_HEADS * V_HEAD_DIM),
        "ffn_w_gate": nrm(ks[12], (DEPTH, D_MODEL, D_FF), D_MODEL),
        "ffn_w_up": nrm(ks[13], (DEPTH, D_MODEL, D_FF), D_MODEL),
        "ffn_w_down": nrm(ks[14], (DEPTH, D_FF, D_MODEL), D_FF),
    }


def reference(x, positions, norm_mix, norm_ffn, norm_final, pool_w, pool_scale,
              mla_w_down, mla_q_norm, mla_w_uq, mla_kv_norm, mla_w_ukv, mla_w_o,
              ffn_w_gate, ffn_w_up, ffn_w_down):
    cos, sin = rope_tables(positions)
    for i in range(DEPTH):
        h = rms_norm(x, norm_mix[i])
        j = i // N_MIXERS
        if i % N_MIXERS == 0:
            x = x + pool_mixer(h, pool_w[j], pool_scale[j])
        else:
            x = x + mla(h, cos, sin, mla_w_down[j], mla_q_norm[j], mla_w_uq[j],
                        mla_kv_norm[j], mla_w_ukv[j], mla_w_o[j])
        h = rms_norm(x, norm_ffn[i])
        x = x + swiglu(h, ffn_w_gate[i], ffn_w_up[i], ffn_w_down[i])
    return rms_norm(x, norm_final)
```

```python
import functools

import numpy as np
import jax
import jax.numpy as jnp
from jax import lax
from jax.experimental import pallas as pl
from jax.experimental.pallas import tpu as pltpu

D_MODEL = 1024
DEPTH = 4
POOL_WINDOWS = (2, 4, 8, 16)
POOL_GROUP = D_MODEL // len(POOL_WINDOWS)
POOL_HALO = 16
N_HEADS = 8
QK_NOPE_DIM = 128
QK_ROPE_DIM = 64
V_HEAD_DIM = 128
Q_LORA_RANK = 384
KV_LORA_RANK = 128
QK_HEAD_DIM = QK_NOPE_DIM + QK_ROPE_DIM
ROPE_THETA = 10000.0
D_FF = 2816
RMS_EPS = 1e-6

LANES = 128
MXU_DIM = 256
FF_CHUNK = MXU_DIM
DOWN_PAD = Q_LORA_RANK + KV_LORA_RANK + LANES

TOKEN_TILE = 512
ATTN_TILE = 512
ROPE_TILE = 1024
VMEM_LIMIT = 56 * 1024 * 1024

BF16 = jnp.bfloat16
F32 = jnp.float32


def _rms(x, g):
    return x * lax.rsqrt(jnp.mean(x * x, axis=-1, keepdims=True) + RMS_EPS) * g


def _dot(a, b):
    return jnp.dot(a, b, preferred_element_type=F32)


def _const_spec(shape):
    return pl.BlockSpec(shape, lambda *_: (0,) * len(shape), pipeline_mode=pl.Buffered(1))


def _params(n_axes):
    return pltpu.CompilerParams(
        dimension_semantics=("arbitrary",) * n_axes,
        vmem_limit_bytes=VMEM_LIMIT)


def _rope_table_kernel(pos_ref, inv_ref, cos_ref, sin_ref):
    ang = pos_ref[...].astype(F32) * inv_ref[...]
    lane = lax.broadcasted_iota(jnp.int32, ang.shape, 1)
    c = jnp.cos(ang)
    s = jnp.sin(ang)
    half = QK_ROPE_DIM // 2
    cos_ref[...] = jnp.where(lane < QK_ROPE_DIM, c, 0.0)
    sin_ref[...] = jnp.where(lane < half, -s, jnp.where(lane < QK_ROPE_DIM, s, 0.0))


def _rope_tables(pos_col):
    n_tok = pos_col.shape[0]
    half = QK_ROPE_DIM // 2
    inv = 1.0 / (ROPE_THETA ** (np.arange(0, QK_ROPE_DIM, 2, dtype=np.float32) / QK_ROPE_DIM))
    inv_row = np.zeros((1, LANES), np.float32)
    inv_row[0, :half] = inv
    inv_row[0, half:QK_ROPE_DIM] = inv
    return pl.pallas_call(
        _rope_table_kernel,
        grid=(n_tok // ROPE_TILE,),
        in_specs=[pl.BlockSpec((ROPE_TILE, 1), lambda i: (i, 0)),
                  _const_spec((1, LANES))],
        out_specs=[pl.BlockSpec((ROPE_TILE, LANES), lambda i: (i, 0))] * 2,
        out_shape=[jax.ShapeDtypeStruct((n_tok, LANES), F32)] * 2,
        compiler_params=_params(1),
        name="rope_tables",
    )(pos_col, jnp.asarray(inv_row))


def _rope(x, cos, sin):
    lane = lax.broadcasted_iota(jnp.int32, x.shape, 1)
    half = QK_ROPE_DIM // 2
    swapped = jnp.where(lane < half,
                        pltpu.roll(x, LANES - half, axis=1),
                        pltpu.roll(x, half, axis=1))
    return x * cos + swapped * sin


def _mla_proj_kernel(x_ref, g_ref, wd_ref, qn_g_ref, kvn_g_ref, wqn_ref, wqr_ref,
                     wkn_ref, wv_ref, cos_ref, sin_ref,
                     qn_ref, qr_ref, kn_ref, kr_ref, v_ref):
    h = _rms(x_ref[...], g_ref[...]).astype(BF16)
    d = _dot(h, wd_ref[...])
    cq = _rms(d[:, :Q_LORA_RANK], qn_g_ref[...]).astype(BF16)
    ckv = _rms(d[:, Q_LORA_RANK:Q_LORA_RANK + KV_LORA_RANK], kvn_g_ref[...]).astype(BF16)
    cos = cos_ref[...]
    sin = sin_ref[...]
    kr_ref[...] = _rope(d[:, Q_LORA_RANK + KV_LORA_RANK:], cos, sin).astype(BF16)
    sm_scale = QK_HEAD_DIM ** -0.5
    qn_ref[...] = (_dot(cq, wqn_ref[...]) * sm_scale).astype(BF16)
    qr = _dot(cq, wqr_ref[...])
    for hd in range(N_HEADS):
        sl = slice(hd * LANES, (hd + 1) * LANES)
        qr_ref[:, sl] = (_rope(qr[:, sl], cos, sin) * sm_scale).astype(BF16)
    kn_ref[...] = _dot(ckv, wkn_ref[...]).astype(BF16)
    v_ref[...] = _dot(ckv, wv_ref[...]).astype(BF16)


def _mla_proj(x, g, wd, qn_g, kvn_g, wqn, wqr, wkn, wv, cos, sin):
    n_tok = x.shape[0]
    tm = TOKEN_TILE
    hv = N_HEADS * LANES
    row = lambda w: pl.BlockSpec((tm, w), lambda i: (i, 0))
    return pl.pallas_call(
        _mla_proj_kernel,
        grid=(n_tok // tm,),
        in_specs=[row(D_MODEL), _const_spec((1, D_MODEL)), _const_spec((D_MODEL, DOWN_PAD)),
                  _const_spec((1, Q_LORA_RANK)), _const_spec((1, KV_LORA_RANK)),
                  _const_spec((Q_LORA_RANK, hv)), _const_spec((Q_LORA_RANK, hv)),
                  _const_spec((KV_LORA_RANK, hv)), _const_spec((KV_LORA_RANK, hv)),
                  row(LANES), row(LANES)],
        out_specs=[row(hv), row(hv), row(hv), row(LANES), row(hv)],
        out_shape=[jax.ShapeDtypeStruct((n_tok, hv), BF16),
                   jax.ShapeDtypeStruct((n_tok, hv), BF16),
                   jax.ShapeDtypeStruct((n_tok, hv), BF16),
                   jax.ShapeDtypeStruct((n_tok, LANES), BF16),
                   jax.ShapeDtypeStruct((n_tok, hv), BF16)],
        compiler_params=_params(1),
        name="mla_proj",
    )(x, g, wd, qn_g, kvn_g, wqn, wqr, wkn, wv, cos, sin)


def _attn_kernel(qn_ref, qr_ref, kn_ref, kr_ref, v_ref, o_ref, m_ref, l_ref, acc_ref):
    qi = pl.program_id(1)
    t = ATTN_TILE
    row = lax.broadcasted_iota(jnp.int32, (t, t), 0)
    col = lax.broadcasted_iota(jnp.int32, (t, t), 1)
    causal = col <= row
    for hd in range(N_HEADS):
        sl = slice(hd * LANES, (hd + 1) * LANES)
        q = jnp.concatenate([qn_ref[:, sl], qr_ref[:, sl]], axis=1)
        m_ref[...] = jnp.full(m_ref.shape, -jnp.inf, F32)
        l_ref[...] = jnp.zeros(l_ref.shape, F32)
        acc_ref[...] = jnp.zeros(acc_ref.shape, F32)

        def step(ki, masked):
            rows = pl.ds(pl.multiple_of(ki * t, t), t)
            k = jnp.concatenate([kn_ref[rows, sl], kr_ref[rows, :]], axis=1)
            s = lax.dot_general(q, k, (((1,), (1,)), ((), ())),
                                preferred_element_type=F32)
            if masked:
                s = jnp.where(causal, s, -jnp.inf)
            m_prev = m_ref[...]
            m_new = jnp.maximum(m_prev, jnp.max(s, axis=1, keepdims=True))
            alpha = jnp.exp(m_prev - m_new)
            p = jnp.exp(s - jnp.concatenate([m_new] * (t // LANES), axis=1))
            l_ref[...] = alpha * l_ref[...] + jnp.sum(p, axis=1, keepdims=True)
            acc_ref[...] = alpha * acc_ref[...] + _dot(p.astype(BF16), v_ref[rows, sl])
            m_ref[...] = m_new

        def body(ki, carry):
            step(ki, False)
            return carry

        lax.fori_loop(0, qi, body, 0)
        step(qi, True)
        o_ref[:, sl] = (acc_ref[...] / l_ref[...]).astype(BF16)


def _attention(qn, qr, kn, kr, v, batch, seq):
    t = ATTN_TILE
    nq = seq // t
    hv = N_HEADS * LANES
    q_spec = pl.BlockSpec((t, hv), lambda b, i: (b * nq + i, 0))
    seq_spec = lambda w: pl.BlockSpec((seq, w), lambda b, i: (b, 0))
    return pl.pallas_call(
        _attn_kernel,
        grid=(batch, nq),
        in_specs=[q_spec, q_spec, seq_spec(hv), seq_spec(LANES), seq_spec(hv)],
        out_specs=q_spec,
        out_shape=jax.ShapeDtypeStruct((batch * seq, hv), BF16),
        scratch_shapes=[pltpu.VMEM((t, LANES), F32)] * 3,
        compiler_params=_params(2),
        name="mla_attention",
    )(qn, qr, kn, kr, v)


def _swiglu_tail(x1, g_ref, wg_ref, wu_ref, wdn_ref, a_ref):
    h = _rms(x1, g_ref[...]).astype(BF16)
    for c in range(D_FF // FF_CHUNK):
        sl = slice(c * FF_CHUNK, (c + 1) * FF_CHUNK)
        gate = _dot(h, wg_ref[:, sl])
        up = _dot(h, wu_ref[:, sl])
        a_ref[:, sl] = (gate / (1.0 + jnp.exp(-gate)) * up).astype(BF16)
    return x1 + _dot(a_ref[...], wdn_ref[...])


def _pool_ffn_kernel(x_ref, halo_ref, g_ref, pw_ref, ps_ref, gf_ref, wg_ref, wu_ref, wdn_ref,
                     gfin_ref, o_ref, ext_ref, x1_ref, a_ref, *, tiles_per_seq, final_norm):
    tm = x_ref.shape[0]
    i = pl.program_id(0)
    blk = i % tiles_per_seq
    x = x_ref[...]
    h = _rms(x, g_ref[...])
    halo = _rms(halo_ref[...], g_ref[...])
    ext_ref[:POOL_HALO, :] = jnp.where(blk == 0, 0.0, halo)
    ext_ref[POOL_HALO:, :] = h
    pos = blk * tm + lax.broadcasted_iota(jnp.int32, (tm, 1), 0)
    avail = (pos + 1).astype(F32)
    for g, win in enumerate(POOL_WINDOWS):
        sl = slice(g * POOL_GROUP, (g + 1) * POOL_GROUP)
        h_g = ext_ref[POOL_HALO:, sl]
        acc = h_g
        for j in range(1, win):
            acc = acc + ext_ref[POOL_HALO - j:POOL_HALO - j + tm, sl]
        p = acc / jnp.minimum(avail, float(win)) - h_g
        y = _dot(p.astype(BF16), pw_ref[g])
        x1_ref[:, sl] = x[:, sl] + y * ps_ref[:, sl]
    out = _swiglu_tail(x1_ref[...], gf_ref, wg_ref, wu_ref, wdn_ref, a_ref)
    if final_norm:
        out = _rms(out, gfin_ref[...])
    o_ref[...] = out


def _wo_ffn_kernel(x_ref, o_attn_ref, wo_ref, gf_ref, wg_ref, wu_ref, wdn_ref,
                   gfin_ref, o_ref, a_ref, *, final_norm):
    x1 = x_ref[...] + _dot(o_attn_ref[...], wo_ref[...])
    out = _swiglu_tail(x1, gf_ref, wg_ref, wu_ref, wdn_ref, a_ref)
    if final_norm:
        out = _rms(out, gfin_ref[...])
    o_ref[...] = out


def _ffn_specs():
    return [_const_spec((1, D_MODEL)), _const_spec((D_MODEL, D_FF)),
            _const_spec((D_MODEL, D_FF)), _const_spec((D_FF, D_MODEL)),
            _const_spec((1, D_MODEL))]


def _pool_ffn(x, seq, g, pw, ps, gf, wg, wu, wdn, gfin, final_norm):
    n_tok = x.shape[0]
    tm = TOKEN_TILE
    halo_per_tile = tm // POOL_HALO
    row = pl.BlockSpec((tm, D_MODEL), lambda i: (i, 0))
    halo = pl.BlockSpec((POOL_HALO, D_MODEL),
                        lambda i: (jnp.maximum(i * halo_per_tile - 1, 0), 0))
    return pl.pallas_call(
        functools.partial(_pool_ffn_kernel, tiles_per_seq=seq // tm, final_norm=final_norm),
        grid=(n_tok // tm,),
        in_specs=[row, halo, _const_spec((1, D_MODEL)),
                  _const_spec((len(POOL_WINDOWS), POOL_GROUP, POOL_GROUP)),
                  _const_spec((1, D_MODEL))] + _ffn_specs(),
        out_specs=row,
        out_shape=jax.ShapeDtypeStruct((n_tok, D_MODEL), F32),
        scratch_shapes=[pltpu.VMEM((tm + POOL_HALO, D_MODEL), F32),
                        pltpu.VMEM((tm, D_MODEL), F32),
                        pltpu.VMEM((tm, D_FF), BF16)],
        compiler_params=_params(1),
        name="pool_ffn",
    )(x, x, g, pw, ps, gf, wg, wu, wdn, gfin)


def _wo_ffn(x, o_attn, wo, gf, wg, wu, wdn, gfin, final_norm):
    n_tok = x.shape[0]
    tm = TOKEN_TILE
    row = lambda dt: pl.BlockSpec((tm, D_MODEL), lambda i: (i, 0))
    return pl.pallas_call(
        functools.partial(_wo_ffn_kernel, final_norm=final_norm),
        grid=(n_tok // tm,),
        in_specs=[row(F32), row(BF16), _const_spec((D_MODEL, D_MODEL))] + _ffn_specs(),
        out_specs=row(F32),
        out_shape=jax.ShapeDtypeStruct((n_tok, D_MODEL), F32),
        scratch_shapes=[pltpu.VMEM((tm, D_FF), BF16)],
        compiler_params=_params(1),
        name="wo_ffn",
    )(x, o_attn, wo, gf, wg, wu, wdn, gfin)


def kernel(x, positions, norm_mix, norm_ffn, norm_final, pool_w, pool_scale, mla_w_down,
           mla_q_norm, mla_w_uq, mla_kv_norm, mla_w_ukv, mla_w_o, ffn_w_gate, ffn_w_up,
           ffn_w_down):
    batch, seq, d = x.shape
    n_tok = batch * seq
    xt = x.reshape(n_tok, d)
    cos, sin = _rope_tables(positions.reshape(n_tok, 1))
    gfin = norm_final.reshape(1, d)
    n_mixers = 2
    for i in range(DEPTH):
        j = i // n_mixers
        final = i == DEPTH - 1
        gm = norm_mix[i].reshape(1, d)
        gf = norm_ffn[i].reshape(1, d)
        wg = ffn_w_gate[i].astype(BF16)
        wu = ffn_w_up[i].astype(BF16)
        wdn = ffn_w_down[i].astype(BF16)
        if i % n_mixers == 0:
            xt = _pool_ffn(xt, seq, gm, pool_w[j].astype(BF16), pool_scale[j].reshape(1, d),
                           gf, wg, wu, wdn, gfin, final)
        else:
            rope_pad = LANES - QK_ROPE_DIM
            wd = jnp.pad(mla_w_down[j], ((0, 0), (0, rope_pad))).astype(BF16)
            wq = mla_w_uq[j].reshape(Q_LORA_RANK, N_HEADS, QK_HEAD_DIM)
            wqn = wq[:, :, :QK_NOPE_DIM].reshape(Q_LORA_RANK, -1).astype(BF16)
            wqr = jnp.pad(wq[:, :, QK_NOPE_DIM:], ((0, 0), (0, 0), (0, rope_pad)))
            wqr = wqr.reshape(Q_LORA_RANK, -1).astype(BF16)
            wkv = mla_w_ukv[j].reshape(KV_LORA_RANK, N_HEADS, QK_NOPE_DIM + V_HEAD_DIM)
            wkn = wkv[:, :, :QK_NOPE_DIM].reshape(KV_LORA_RANK, -1).astype(BF16)
            wv = wkv[:, :, QK_NOPE_DIM:].reshape(KV_LORA_RANK, -1).astype(BF16)
            qn, qr, kn, kr, v = _mla_proj(
                xt, gm, wd, mla_q_norm[j].reshape(1, -1), mla_kv_norm[j].reshape(1, -1),
                wqn, wqr, wkn, wv, cos, sin)
            o_attn = _attention(qn, qr, kn, kr, v, batch, seq)
            xt = _wo_ffn(xt, o_attn, mla_w_o[j].astype(BF16), gf, wg, wu, wdn, gfin, final)
    return xt.reshape(batch, seq, d)
```

```python
import functools

import numpy as np
import jax
import jax.numpy as jnp
from jax import lax
from jax.experimental import pallas as pl
from jax.experimental.pallas import tpu as pltpu

D_MODEL = 1024
DEPTH = 4
POOL_WINDOWS = (2, 4, 8, 16)
POOL_GROUP = D_MODEL // len(POOL_WINDOWS)
POOL_HALO = 16
N_HEADS = 8
QK_NOPE_DIM = 128
QK_ROPE_DIM = 64
V_HEAD_DIM = 128
Q_LORA_RANK = 384
KV_LORA_RANK = 128
QK_HEAD_DIM = QK_NOPE_DIM + QK_ROPE_DIM
ROPE_THETA = 10000.0
D_FF = 2816
RMS_EPS = 1e-6

LANES = 128
MXU_DIM = 256
FF_CHUNK = MXU_DIM
DOWN_PAD = Q_LORA_RANK + KV_LORA_RANK + LANES

TOKEN_TILE = 512
ATTN_TILE = 512
ROPE_TILE = 1024
VMEM_LIMIT = 56 * 1024 * 1024

LOG2_E = 1.4426950408889634
BF16 = jnp.bfloat16
F32 = jnp.float32


def _rms(x, g):
    return x * lax.rsqrt(jnp.mean(x * x, axis=-1, keepdims=True) + RMS_EPS) * g


def _dot(a, b):
    return jnp.dot(a, b, preferred_element_type=F32)


def _const_spec(shape):
    return pl.BlockSpec(shape, lambda *_: (0,) * len(shape), pipeline_mode=pl.Buffered(1))


def _params(n_axes):
    return pltpu.CompilerParams(
        dimension_semantics=("arbitrary",) * n_axes,
        vmem_limit_bytes=VMEM_LIMIT)


def _rope_table_kernel(pos_ref, inv_ref, cos_ref, sin_ref):
    ang = pos_ref[...].astype(F32) * inv_ref[...]
    lane = lax.broadcasted_iota(jnp.int32, ang.shape, 1)
    c = jnp.cos(ang)
    s = jnp.sin(ang)
    half = QK_ROPE_DIM // 2
    cos_ref[...] = jnp.where(lane < QK_ROPE_DIM, c, 0.0)
    sin_ref[...] = jnp.where(lane < half, -s, jnp.where(lane < QK_ROPE_DIM, s, 0.0))


def _rope_tables(pos_col):
    n_tok = pos_col.shape[0]
    half = QK_ROPE_DIM // 2
    inv = 1.0 / (ROPE_THETA ** (np.arange(0, QK_ROPE_DIM, 2, dtype=np.float32) / QK_ROPE_DIM))
    inv_row = np.zeros((1, LANES), np.float32)
    inv_row[0, :half] = inv
    inv_row[0, half:QK_ROPE_DIM] = inv
    return pl.pallas_call(
        _rope_table_kernel,
        grid=(n_tok // ROPE_TILE,),
        in_specs=[pl.BlockSpec((ROPE_TILE, 1), lambda i: (i, 0)),
                  _const_spec((1, LANES))],
        out_specs=[pl.BlockSpec((ROPE_TILE, LANES), lambda i: (i, 0))] * 2,
        out_shape=[jax.ShapeDtypeStruct((n_tok, LANES), F32)] * 2,
        compiler_params=_params(1),
        name="rope_tables",
    )(pos_col, jnp.asarray(inv_row))


def _rope(x, cos, sin):
    lane = lax.broadcasted_iota(jnp.int32, x.shape, 1)
    half = QK_ROPE_DIM // 2
    swapped = jnp.where(lane < half,
                        pltpu.roll(x, LANES - half, axis=1),
                        pltpu.roll(x, half, axis=1))
    return x * cos + swapped * sin


def _mla_proj_kernel(x_ref, g_ref, wd_ref, qn_g_ref, kvn_g_ref, wqn_ref, wqr_ref,
                     wkn_ref, wv_ref, cos_ref, sin_ref,
                     qn_ref, qr_ref, kn_ref, kr_ref, v_ref):
    h = _rms(x_ref[...], g_ref[...]).astype(BF16)
    d = _dot(h, wd_ref[...])
    cq = _rms(d[:, :Q_LORA_RANK], qn_g_ref[...]).astype(BF16)
    ckv = _rms(d[:, Q_LORA_RANK:Q_LORA_RANK + KV_LORA_RANK], kvn_g_ref[...]).astype(BF16)
    cos = cos_ref[...]
    sin = sin_ref[...]
    kr_ref[...] = _rope(d[:, Q_LORA_RANK + KV_LORA_RANK:], cos, sin).astype(BF16)
    sm_scale = QK_HEAD_DIM ** -0.5 * LOG2_E
    qn_ref[...] = (_dot(cq, wqn_ref[...]) * sm_scale).astype(BF16)
    qr = _dot(cq, wqr_ref[...])
    for hd in range(N_HEADS):
        sl = slice(hd * LANES, (hd + 1) * LANES)
        qr_ref[:, sl] = (_rope(qr[:, sl], cos, sin) * sm_scale).astype(BF16)
    kn_ref[...] = _dot(ckv, wkn_ref[...]).astype(BF16)
    v_ref[...] = _dot(ckv, wv_ref[...]).astype(BF16)


def _mla_proj(x, g, wd, qn_g, kvn_g, wqn, wqr, wkn, wv, cos, sin):
    n_tok = x.shape[0]
    tm = TOKEN_TILE
    hv = N_HEADS * LANES
    row = lambda w: pl.BlockSpec((tm, w), lambda i: (i, 0))
    return pl.pallas_call(
        _mla_proj_kernel,
        grid=(n_tok // tm,),
        in_specs=[row(D_MODEL), _const_spec((1, D_MODEL)), _const_spec((D_MODEL, DOWN_PAD)),
                  _const_spec((1, Q_LORA_RANK)), _const_spec((1, KV_LORA_RANK)),
                  _const_spec((Q_LORA_RANK, hv)), _const_spec((Q_LORA_RANK, hv)),
                  _const_spec((KV_LORA_RANK, hv)), _const_spec((KV_LORA_RANK, hv)),
                  row(LANES), row(LANES)],
        out_specs=[row(hv), row(hv), row(hv), row(LANES), row(hv)],
        out_shape=[jax.ShapeDtypeStruct((n_tok, hv), BF16),
                   jax.ShapeDtypeStruct((n_tok, hv), BF16),
                   jax.ShapeDtypeStruct((n_tok, hv), BF16),
                   jax.ShapeDtypeStruct((n_tok, LANES), BF16),
                   jax.ShapeDtypeStruct((n_tok, hv), BF16)],
        compiler_params=_params(1),
        name="mla_proj",
    )(x, g, wd, qn_g, kvn_g, wqn, wqr, wkn, wv, cos, sin)


def _attn_kernel(qn_ref, qr_ref, kn_ref, kr_ref, v_ref, o_ref, m_ref, acc_ref):
    qi = pl.program_id(1)
    t = ATTN_TILE
    m_ref[...] = jnp.full(m_ref.shape, -jnp.inf, F32)
    acc_ref[...] = jnp.zeros(acc_ref.shape, F32)
    ones = jnp.ones((t, LANES), BF16)

    def step(ki, masked):
        rows = pl.ds(pl.multiple_of(ki * t, t), t)
        k_rope = kr_ref[rows, :]
        if masked:
            row = lax.broadcasted_iota(jnp.int32, (t, t), 0)
            col = lax.broadcasted_iota(jnp.int32, (t, t), 1)
            causal = col <= row
        for hd in range(N_HEADS):
            sl = slice(hd * LANES, (hd + 1) * LANES)
            q = jnp.concatenate([qn_ref[:, sl], qr_ref[:, sl]], axis=1)
            k = jnp.concatenate([kn_ref[rows, sl], k_rope], axis=1)
            s = lax.dot_general(q, k, (((1,), (1,)), ((), ())),
                                preferred_element_type=F32)
            if masked:
                s = jnp.where(causal, s, -jnp.inf)
            m_prev = m_ref[hd]
            m_new = jnp.maximum(m_prev, jnp.max(s, axis=1, keepdims=True))
            alpha = jnp.exp2(m_prev - m_new)
            p = jnp.exp2(s - jnp.concatenate([m_new] * (t // LANES), axis=1))
            v_ext = jnp.concatenate([v_ref[rows, sl], ones], axis=1)
            acc_ref[hd] = (jnp.concatenate([alpha, alpha], axis=1) * acc_ref[hd]
                           + _dot(p.astype(BF16), v_ext))
            m_ref[hd] = m_new

    def body(ki, carry):
        step(ki, False)
        return carry

    lax.fori_loop(0, qi, body, 0)
    step(qi, True)
    for hd in range(N_HEADS):
        acc = acc_ref[hd]
        o_ref[:, hd * LANES:(hd + 1) * LANES] = (acc[:, :LANES] / acc[:, LANES:]).astype(BF16)


def _attention(qn, qr, kn, kr, v, batch, seq):
    t = ATTN_TILE
    nq = seq // t
    hv = N_HEADS * LANES
    q_spec = pl.BlockSpec((t, hv), lambda b, i: (b * nq + i, 0))
    seq_spec = lambda w: pl.BlockSpec((seq, w), lambda b, i: (b, 0))
    return pl.pallas_call(
        _attn_kernel,
        grid=(batch, nq),
        in_specs=[q_spec, q_spec, seq_spec(hv), seq_spec(LANES), seq_spec(hv)],
        out_specs=q_spec,
        out_shape=jax.ShapeDtypeStruct((batch * seq, hv), BF16),
        scratch_shapes=[pltpu.VMEM((N_HEADS, t, LANES), F32),
                        pltpu.VMEM((N_HEADS, t, 2 * LANES), F32)],
        compiler_params=_params(2),
        name="mla_attention",
    )(qn, qr, kn, kr, v)


def _ffn_stage(x1_ref, h_ref, wg_ref, wu_ref, wdn_ref, gfin_ref, a_ref, o_ref, final_norm,
               side_work):
    h = h_ref[...]
    side_work = list(side_work)
    for c in range(D_FF // FF_CHUNK):
        sl = slice(c * FF_CHUNK, (c + 1) * FF_CHUNK)
        gate = _dot(h, wg_ref[:, sl])
        up = _dot(h, wu_ref[:, sl])
        a_ref[:, sl] = (gate / (1.0 + jnp.exp(-gate)) * up).astype(BF16)
        if side_work:
            side_work.pop(0)()
    for piece in side_work:
        piece()
    out = x1_ref[...] + _dot(a_ref[...], wdn_ref[...])
    if final_norm:
        out = _rms(out, gfin_ref[...])
    o_ref[...] = out


def _pipelined_step(mixer_pieces, gf_ref, ffn_refs, o_ref, x1_slots, h_slots, final_norm):
    i = pl.program_id(0)

    @pl.when(i == 0)
    def _():
        x1_slots[1][...] = jnp.zeros(x1_slots[1].shape, F32)
        h_slots[1][...] = jnp.zeros(h_slots[1].shape, BF16)

    for cur in (0, 1):
        @pl.when(i % 2 == cur)
        def _():
            def ffn_norm():
                h_slots[cur][...] = _rms(x1_slots[cur][...], gf_ref[...]).astype(BF16)

            _ffn_stage(x1_slots[1 - cur], h_slots[1 - cur], *ffn_refs, o_ref, final_norm,
                       mixer_pieces(x1_slots[cur]) + [ffn_norm])


def _pool_ffn_kernel(x_ref, halo_ref, g_ref, pw_ref, ps_ref, gf_ref, wg_ref, wu_ref, wdn_ref,
                     gfin_ref, o_ref, ext_ref, a_ref, x1_a, x1_b, h_a, h_b,
                     *, n_tiles, tiles_per_seq, final_norm):
    tm = x_ref.shape[0]
    blk = jnp.minimum(pl.program_id(0), n_tiles - 1) % tiles_per_seq

    def mixer_pieces(x1_ref):
        def norm():
            ext_ref[POOL_HALO:, :] = _rms(x_ref[...], g_ref[...])
            halo = _rms(halo_ref[...], g_ref[...])
            ext_ref[:POOL_HALO, :] = jnp.where(blk == 0, 0.0, halo)

        def group(g, win):
            sl = slice(g * POOL_GROUP, (g + 1) * POOL_GROUP)
            pos = blk * tm + lax.broadcasted_iota(jnp.int32, (tm, 1), 0)
            avail = (pos + 1).astype(F32)
            h_g = ext_ref[POOL_HALO:, sl]
            acc = h_g
            for j in range(1, win):
                acc = acc + ext_ref[POOL_HALO - j:POOL_HALO - j + tm, sl]
            p = acc / jnp.minimum(avail, float(win)) - h_g
            y = _dot(p.astype(BF16), pw_ref[g])
            x1_ref[:, sl] = x_ref[:, sl] + y * ps_ref[:, sl]

        return [norm] + [functools.partial(group, g, win)
                         for g, win in enumerate(POOL_WINDOWS)]

    _pipelined_step(mixer_pieces, gf_ref, (wg_ref, wu_ref, wdn_ref, gfin_ref, a_ref), o_ref,
                    (x1_a, x1_b), (h_a, h_b), final_norm)


def _wo_ffn_kernel(x_ref, o_attn_ref, wo_ref, gf_ref, wg_ref, wu_ref, wdn_ref,
                   gfin_ref, o_ref, a_ref, x1_a, x1_b, h_a, h_b, *, final_norm):
    def mixer_pieces(x1_ref):
        def out_proj():
            x1_ref[...] = x_ref[...] + _dot(o_attn_ref[...], wo_ref[...])

        return [out_proj]

    _pipelined_step(mixer_pieces, gf_ref, (wg_ref, wu_ref, wdn_ref, gfin_ref, a_ref), o_ref,
                    (x1_a, x1_b), (h_a, h_b), final_norm)


def _ffn_specs():
    return [_const_spec((1, D_MODEL)), _const_spec((D_MODEL, D_FF)),
            _const_spec((D_MODEL, D_FF)), _const_spec((D_FF, D_MODEL)),
            _const_spec((1, D_MODEL))]


def _pipeline_scratch(tm):
    return [pltpu.VMEM((tm, D_FF), BF16),
            pltpu.VMEM((tm, D_MODEL), F32), pltpu.VMEM((tm, D_MODEL), F32),
            pltpu.VMEM((tm, D_MODEL), BF16), pltpu.VMEM((tm, D_MODEL), BF16)]


def _pool_ffn(x, seq, g, pw, ps, gf, wg, wu, wdn, gfin, final_norm):
    n_tok = x.shape[0]
    tm = TOKEN_TILE
    n_tiles = n_tok // tm
    halo_per_tile = tm // POOL_HALO
    tile_in = lambda i: jnp.minimum(i, n_tiles - 1)
    row_in = pl.BlockSpec((tm, D_MODEL), lambda i: (tile_in(i), 0))
    halo = pl.BlockSpec((POOL_HALO, D_MODEL),
                        lambda i: (jnp.maximum(tile_in(i) * halo_per_tile - 1, 0), 0))
    row_out = pl.BlockSpec((tm, D_MODEL), lambda i: (jnp.maximum(i - 1, 0), 0))
    return pl.pallas_call(
        functools.partial(_pool_ffn_kernel, n_tiles=n_tiles, tiles_per_seq=seq // tm,
                          final_norm=final_norm),
        grid=(n_tiles + 1,),
        in_specs=[row_in, halo, _const_spec((1, D_MODEL)),
                  _const_spec((len(POOL_WINDOWS), POOL_GROUP, POOL_GROUP)),
                  _const_spec((1, D_MODEL))] + _ffn_specs(),
        out_specs=row_out,
        out_shape=jax.ShapeDtypeStruct((n_tok, D_MODEL), F32),
        scratch_shapes=[pltpu.VMEM((tm + POOL_HALO, D_MODEL), F32)] + _pipeline_scratch(tm),
        compiler_params=_params(1),
        name="pool_ffn",
    )(x, x, g, pw, ps, gf, wg, wu, wdn, gfin)


def _wo_ffn(x, o_attn, wo, gf, wg, wu, wdn, gfin, final_norm):
    n_tok = x.shape[0]
    tm = TOKEN_TILE
    n_tiles = n_tok // tm
    row_in = pl.BlockSpec((tm, D_MODEL), lambda i: (jnp.minimum(i, n_tiles - 1), 0))
    row_out = pl.BlockSpec((tm, D_MODEL), lambda i: (jnp.maximum(i - 1, 0), 0))
    return pl.pallas_call(
        functools.partial(_wo_ffn_kernel, final_norm=final_norm),
        grid=(n_tiles + 1,),
        in_specs=[row_in, row_in, _const_spec((D_MODEL, D_MODEL))] + _ffn_specs(),
        out_specs=row_out,
        out_shape=jax.ShapeDtypeStruct((n_tok, D_MODEL), F32),
        scratch_shapes=_pipeline_scratch(tm),
        compiler_params=_params(1),
        name="wo_ffn",
    )(x, o_attn, wo, gf, wg, wu, wdn, gfin)


def kernel(x, positions, norm_mix, norm_ffn, norm_final, pool_w, pool_scale, mla_w_down,
           mla_q_norm, mla_w_uq, mla_kv_norm, mla_w_ukv, mla_w_o, ffn_w_gate, ffn_w_up,
           ffn_w_down):
    batch, seq, d = x.shape
    n_tok = batch * seq
    xt = x.reshape(n_tok, d)
    cos, sin = _rope_tables(positions.reshape(n_tok, 1))
    gfin = norm_final.reshape(1, d)
    n_mixers = 2
    for i in range(DEPTH):
        j = i // n_mixers
        final = i == DEPTH - 1
        gm = norm_mix[i].reshape(1, d)
        gf = norm_ffn[i].reshape(1, d)
        wg = ffn_w_gate[i].astype(BF16)
        wu = ffn_w_up[i].astype(BF16)
        wdn = ffn_w_down[i].astype(BF16)
        if i % n_mixers == 0:
            xt = _pool_ffn(xt, seq, gm, pool_w[j].astype(BF16), pool_scale[j].reshape(1, d),
                           gf, wg, wu, wdn, gfin, final)
        else:
            rope_pad = LANES - QK_ROPE_DIM
            wd = jnp.pad(mla_w_down[j], ((0, 0), (0, rope_pad))).astype(BF16)
            wq = mla_w_uq[j].reshape(Q_LORA_RANK, N_HEADS, QK_HEAD_DIM)
            wqn = wq[:, :, :QK_NOPE_DIM].reshape(Q_LORA_RANK, -1).astype(BF16)
            wqr = jnp.pad(wq[:, :, QK_NOPE_DIM:], ((0, 0), (0, 0), (0, rope_pad)))
            wqr = wqr.reshape(Q_LORA_RANK, -1).astype(BF16)
            wkv = mla_w_ukv[j].reshape(KV_LORA_RANK, N_HEADS, QK_NOPE_DIM + V_HEAD_DIM)
            wkn = wkv[:, :, :QK_NOPE_DIM].reshape(KV_LORA_RANK, -1).astype(BF16)
            wv = wkv[:, :, QK_NOPE_DIM:].reshape(KV_LORA_RANK, -1).astype(BF16)
            qn, qr, kn, kr, v = _mla_proj(
                xt, gm, wd, mla_q_norm[j].reshape(1, -1), mla_kv_norm[j].reshape(1, -1),
                wqn, wqr, wkn, wv, cos, sin)
            o_attn = _attention(qn, qr, kn, kr, v, batch, seq)
            xt = _wo_ffn(xt, o_attn, mla_w_o[j].astype(BF16), gf, wg, wu, wdn, gfin, final)
    return xt.reshape(batch, seq, d)
```

```python
import functools

import numpy as np
import jax
import jax.numpy as jnp
from jax import lax
from jax.experimental import pallas as pl
from jax.experimental.pallas import tpu as pltpu

D_MODEL = 1024
DEPTH = 4
POOL_WINDOWS = (2, 4, 8, 16)
POOL_GROUP = D_MODEL // len(POOL_WINDOWS)
POOL_HALO = 16
N_HEADS = 8
QK_NOPE_DIM = 128
QK_ROPE_DIM = 64
V_HEAD_DIM = 128
Q_LORA_RANK = 384
KV_LORA_RANK = 128
QK_HEAD_DIM = QK_NOPE_DIM + QK_ROPE_DIM
ROPE_THETA = 10000.0
D_FF = 2816
RMS_EPS = 1e-6

LANES = 128
MXU_DIM = 256
FF_CHUNK = MXU_DIM
DOWN_PAD = Q_LORA_RANK + KV_LORA_RANK + LANES

TOKEN_TILE = 512
ATTN_TILE = 512
PROJ_TILE = 1024
PROJ_SUB_TILE = 512
ROPE_TILE = 1024
VMEM_LIMIT = 56 * 1024 * 1024

LOG2_E = 1.4426950408889634
BF16 = jnp.bfloat16
F32 = jnp.float32


def _rms(x, g):
    return x * lax.rsqrt(jnp.mean(x * x, axis=-1, keepdims=True) + RMS_EPS) * g


def _dot(a, b):
    return jnp.dot(a, b, preferred_element_type=F32)


def _const_spec(shape):
    return pl.BlockSpec(shape, lambda *_: (0,) * len(shape), pipeline_mode=pl.Buffered(1))


def _params(n_axes):
    return pltpu.CompilerParams(
        dimension_semantics=("arbitrary",) * n_axes,
        vmem_limit_bytes=VMEM_LIMIT)


def _rope_table_kernel(pos_ref, inv_ref, cos_ref, sin_ref):
    ang = pos_ref[...].astype(F32) * inv_ref[...]
    cos_ref[...] = jnp.cos(ang)
    sin_ref[...] = jnp.sin(ang)


def _rope_tables(positions):
    n_tok = positions.shape[0]
    half = QK_ROPE_DIM // 2
    tok_per_row = LANES // half
    n_rows = n_tok // tok_per_row
    inv = 1.0 / (ROPE_THETA ** (np.arange(0, QK_ROPE_DIM, 2, dtype=np.float32) / QK_ROPE_DIM))
    inv_row = np.tile(inv, tok_per_row)[None, :]
    pos_rows = jnp.repeat(positions.reshape(n_rows, tok_per_row), half, axis=1)
    spec = pl.BlockSpec((ROPE_TILE, LANES), lambda i: (i, 0))
    cos, sin = pl.pallas_call(
        _rope_table_kernel,
        grid=(n_rows // ROPE_TILE,),
        in_specs=[spec, _const_spec((1, LANES))],
        out_specs=[spec] * 2,
        out_shape=[jax.ShapeDtypeStruct((n_rows, LANES), F32)] * 2,
        compiler_params=_params(1),
        name="rope_tables",
    )(pos_rows, jnp.asarray(inv_row))
    return cos.reshape(n_tok, half), sin.reshape(n_tok, half)


def _rope(x, cos, sin):
    lane = lax.broadcasted_iota(jnp.int32, x.shape, 1)
    half = QK_ROPE_DIM // 2
    swapped = jnp.where(lane < half,
                        pltpu.roll(x, LANES - half, axis=1),
                        pltpu.roll(x, half, axis=1))
    return x * cos + swapped * sin


def _mla_proj_kernel(x_ref, g_ref, wd_ref, qn_g_ref, kvn_g_ref, wqn_ref, wqr_ref,
                     wkn_ref, wv_ref, cos_ref, sin_ref,
                     qn_ref, qr_ref, kn_ref, kr_ref, v_ref):
    tm = x_ref.shape[0]
    subs = [pl.ds(r, PROJ_SUB_TILE) for r in range(0, tm, PROJ_SUB_TILE)]
    sm_scale = QK_HEAD_DIM ** -0.5 * LOG2_E
    pad = jnp.zeros((PROJ_SUB_TILE, LANES - QK_ROPE_DIM), F32)
    h = [_rms(x_ref[r, :], g_ref[...]).astype(BF16) for r in subs]
    d = [_dot(hh, wd_ref[...]) for hh in h]
    cq = [_rms(dd[:, :Q_LORA_RANK], qn_g_ref[...]).astype(BF16) for dd in d]
    ckv = [_rms(dd[:, Q_LORA_RANK:Q_LORA_RANK + KV_LORA_RANK], kvn_g_ref[...]).astype(BF16)
           for dd in d]
    cos = [jnp.concatenate([cos_ref[r, :], cos_ref[r, :], pad], axis=1) for r in subs]
    sin = [jnp.concatenate([-sin_ref[r, :], sin_ref[r, :], pad], axis=1) for r in subs]
    for i, r in enumerate(subs):
        kr_ref[r, :] = _rope(d[i][:, Q_LORA_RANK + KV_LORA_RANK:], cos[i], sin[i]).astype(BF16)
    for i, r in enumerate(subs):
        qn_ref[r, :] = (_dot(cq[i], wqn_ref[...]) * sm_scale).astype(BF16)
    qr = [_dot(c, wqr_ref[...]) for c in cq]
    for i, r in enumerate(subs):
        for hd in range(N_HEADS):
            sl = slice(hd * LANES, (hd + 1) * LANES)
            qr_ref[r, sl] = (_rope(qr[i][:, sl], cos[i], sin[i]) * sm_scale).astype(BF16)
    for i, r in enumerate(subs):
        kn_ref[r, :] = _dot(ckv[i], wkn_ref[...]).astype(BF16)
    for i, r in enumerate(subs):
        v_ref[r, :] = _dot(ckv[i], wv_ref[...]).astype(BF16)


def _mla_proj(x, g, wd, qn_g, kvn_g, wqn, wqr, wkn, wv, cos, sin):
    n_tok = x.shape[0]
    tm = PROJ_TILE
    hv = N_HEADS * LANES
    row = lambda w: pl.BlockSpec((tm, w), lambda i: (i, 0))
    return pl.pallas_call(
        _mla_proj_kernel,
        grid=(n_tok // tm,),
        in_specs=[row(D_MODEL), _const_spec((1, D_MODEL)), _const_spec((D_MODEL, DOWN_PAD)),
                  _const_spec((1, Q_LORA_RANK)), _const_spec((1, KV_LORA_RANK)),
                  _const_spec((Q_LORA_RANK, hv)), _const_spec((Q_LORA_RANK, hv)),
                  _const_spec((KV_LORA_RANK, hv)), _const_spec((KV_LORA_RANK, hv)),
                  row(QK_ROPE_DIM // 2), row(QK_ROPE_DIM // 2)],
        out_specs=[row(hv), row(hv), row(hv), row(LANES), row(hv)],
        out_shape=[jax.ShapeDtypeStruct((n_tok, hv), BF16),
                   jax.ShapeDtypeStruct((n_tok, hv), BF16),
                   jax.ShapeDtypeStruct((n_tok, hv), BF16),
                   jax.ShapeDtypeStruct((n_tok, LANES), BF16),
                   jax.ShapeDtypeStruct((n_tok, hv), BF16)],
        compiler_params=_params(1),
        name="mla_proj",
    )(x, g, wd, qn_g, kvn_g, wqn, wqr, wkn, wv, cos, sin)


def _attn_kernel(qn_ref, qr_ref, kn_ref, kr_ref, v_ref, o_ref, m_ref, acc_ref, s_ref):
    qi = pl.program_id(1)
    t = ATTN_TILE
    m_ref[...] = jnp.full(m_ref.shape, -jnp.inf, F32)
    acc_ref[...] = jnp.zeros(acc_ref.shape, F32)
    ones = jnp.ones((t, LANES), BF16)

    def tile_rows(ki):
        return pl.ds(pl.multiple_of(ki * t, t), t)

    def scores(hd, ki):
        sl = slice(hd * LANES, (hd + 1) * LANES)
        rows = tile_rows(ki)
        q = jnp.concatenate([qn_ref[:, sl], qr_ref[:, sl]], axis=1)
        k = jnp.concatenate([kn_ref[rows, sl], kr_ref[rows, :]], axis=1)
        return lax.dot_general(q, k, (((1,), (1,)), ((), ())),
                               preferred_element_type=F32)

    def update(hd, ki, s, masked):
        sl = slice(hd * LANES, (hd + 1) * LANES)
        if masked:
            row = lax.broadcasted_iota(jnp.int32, (t, t), 0)
            col = lax.broadcasted_iota(jnp.int32, (t, t), 1)
            s = jnp.where(col <= row, s, -jnp.inf)
        m_prev = m_ref[hd]
        m_new = jnp.maximum(m_prev, jnp.max(s, axis=1, keepdims=True))
        alpha = jnp.exp2(m_prev - m_new)
        p = jnp.exp2(s - jnp.concatenate([m_new] * (t // LANES), axis=1))
        v_ext = jnp.concatenate([v_ref[tile_rows(ki), sl], ones], axis=1)
        acc_ref[hd] = (jnp.concatenate([alpha, alpha], axis=1) * acc_ref[hd]
                       + _dot(p.astype(BF16), v_ext))
        m_ref[hd] = m_new

    def sweep(ki, masked):
        s = s_ref[...]
        for hd in range(N_HEADS):
            s_next = None
            if hd + 1 < N_HEADS:
                s_next = scores(hd + 1, ki)
            elif not masked:
                s_next = scores(0, ki + 1)
            update(hd, ki, s, masked)
            s = s_next
        if not masked:
            s_ref[...] = s

    def body(ki, carry):
        sweep(ki, False)
        return carry

    s_ref[...] = scores(0, 0)
    lax.fori_loop(0, qi, body, 0)
    sweep(qi, True)
    for hd in range(N_HEADS):
        acc = acc_ref[hd]
        o_ref[:, hd * LANES:(hd + 1) * LANES] = (acc[:, :LANES] / acc[:, LANES:]).astype(BF16)


def _attention(qn, qr, kn, kr, v, batch, seq):
    t = ATTN_TILE
    nq = seq // t
    hv = N_HEADS * LANES
    q_spec = pl.BlockSpec((t, hv), lambda b, i: (b * nq + i, 0))
    seq_spec = lambda w: pl.BlockSpec((seq, w), lambda b, i: (b, 0))
    return pl.pallas_call(
        _attn_kernel,
        grid=(batch, nq),
        in_specs=[q_spec, q_spec, seq_spec(hv), seq_spec(LANES), seq_spec(hv)],
        out_specs=q_spec,
        out_shape=jax.ShapeDtypeStruct((batch * seq, hv), BF16),
        scratch_shapes=[pltpu.VMEM((N_HEADS, t, LANES), F32),
                        pltpu.VMEM((N_HEADS, t, 2 * LANES), F32),
                        pltpu.VMEM((t, t), F32)],
        compiler_params=_params(2),
        name="mla_attention",
    )(qn, qr, kn, kr, v)


def _ffn_stage(x1_ref, h_ref, wg_ref, wu_ref, wdn_ref, gfin_ref, a_ref, o_ref, final_norm,
               side_work):
    h = h_ref[...]
    side_work = list(side_work)
    for c in range(D_FF // FF_CHUNK):
        sl = slice(c * FF_CHUNK, (c + 1) * FF_CHUNK)
        gate = _dot(h, wg_ref[:, sl])
        up = _dot(h, wu_ref[:, sl])
        a_ref[:, sl] = (gate / (1.0 + jnp.exp(-gate)) * up).astype(BF16)
        if side_work:
            side_work.pop(0)()
    for piece in side_work:
        piece()
    out = x1_ref[...] + _dot(a_ref[...], wdn_ref[...])
    if final_norm:
        out = _rms(out, gfin_ref[...])
    o_ref[...] = out


def _pipelined_step(mixer_pieces, gf_ref, ffn_refs, o_ref, x1_slots, h_slots, final_norm):
    i = pl.program_id(0)

    @pl.when(i == 0)
    def _():
        x1_slots[1][...] = jnp.zeros(x1_slots[1].shape, F32)
        h_slots[1][...] = jnp.zeros(h_slots[1].shape, BF16)

    for cur in (0, 1):
        @pl.when(i % 2 == cur)
        def _():
            def ffn_norm():
                h_slots[cur][...] = _rms(x1_slots[cur][...], gf_ref[...]).astype(BF16)

            _ffn_stage(x1_slots[1 - cur], h_slots[1 - cur], *ffn_refs, o_ref, final_norm,
                       mixer_pieces(x1_slots[cur]) + [ffn_norm])


def _pool_ffn_kernel(x_ref, halo_ref, g_ref, pw_ref, ps_ref, gf_ref, wg_ref, wu_ref, wdn_ref,
                     gfin_ref, o_ref, ext_ref, a_ref, x1_a, x1_b, h_a, h_b,
                     *, n_tiles, tiles_per_seq, final_norm):
    tm = x_ref.shape[0]
    blk = jnp.minimum(pl.program_id(0), n_tiles - 1) % tiles_per_seq

    def mixer_pieces(x1_ref):
        def norm():
            ext_ref[POOL_HALO:, :] = _rms(x_ref[...], g_ref[...])
            halo = _rms(halo_ref[...], g_ref[...])
            ext_ref[:POOL_HALO, :] = jnp.where(blk == 0, 0.0, halo)

        def group(g, win):
            sl = slice(g * POOL_GROUP, (g + 1) * POOL_GROUP)
            pos = blk * tm + lax.broadcasted_iota(jnp.int32, (tm, 1), 0)
            avail = (pos + 1).astype(F32)
            h_g = ext_ref[POOL_HALO:, sl]
            acc = h_g
            for j in range(1, win):
                acc = acc + ext_ref[POOL_HALO - j:POOL_HALO - j + tm, sl]
            p = acc / jnp.minimum(avail, float(win)) - h_g
            y = _dot(p.astype(BF16), pw_ref[g])
            x1_ref[:, sl] = x_ref[:, sl] + y * ps_ref[:, sl]

        return [norm] + [functools.partial(group, g, win)
                         for g, win in enumerate(POOL_WINDOWS)]

    _pipelined_step(mixer_pieces, gf_ref, (wg_ref, wu_ref, wdn_ref, gfin_ref, a_ref), o_ref,
                    (x1_a, x1_b), (h_a, h_b), final_norm)


def _wo_ffn_kernel(x_ref, o_attn_ref, wo_ref, gf_ref, wg_ref, wu_ref, wdn_ref,
                   gfin_ref, o_ref, a_ref, x1_a, x1_b, h_a, h_b, *, final_norm):
    def mixer_pieces(x1_ref):
        def out_proj():
            x1_ref[...] = x_ref[...] + _dot(o_attn_ref[...], wo_ref[...])

        return [out_proj]

    _pipelined_step(mixer_pieces, gf_ref, (wg_ref, wu_ref, wdn_ref, gfin_ref, a_ref), o_ref,
                    (x1_a, x1_b), (h_a, h_b), final_norm)


def _layer_spec(layer, shape):
    return pl.BlockSpec((None,) + shape, lambda *_: (layer,) + (0,) * len(shape),
                        pipeline_mode=pl.Buffered(1))


def _ffn_specs(layer):
    return [_const_spec((1, D_MODEL)), _layer_spec(layer, (D_MODEL, D_FF)),
            _layer_spec(layer, (D_MODEL, D_FF)), _layer_spec(layer, (D_FF, D_MODEL)),
            _const_spec((1, D_MODEL))]


def _pipeline_scratch(tm):
    return [pltpu.VMEM((tm, D_FF), BF16),
            pltpu.VMEM((tm, D_MODEL), F32), pltpu.VMEM((tm, D_MODEL), F32),
            pltpu.VMEM((tm, D_MODEL), BF16), pltpu.VMEM((tm, D_MODEL), BF16)]


def _pool_ffn(x, seq, g, pw, ps, gf, wg, wu, wdn, gfin, layer, final_norm):
    n_tok = x.shape[0]
    tm = TOKEN_TILE
    n_tiles = n_tok // tm
    halo_per_tile = tm // POOL_HALO
    tile_in = lambda i: jnp.minimum(i, n_tiles - 1)
    row_in = pl.BlockSpec((tm, D_MODEL), lambda i: (tile_in(i), 0))
    halo = pl.BlockSpec((POOL_HALO, D_MODEL),
                        lambda i: (jnp.maximum(tile_in(i) * halo_per_tile - 1, 0), 0))
    row_out = pl.BlockSpec((tm, D_MODEL), lambda i: (jnp.maximum(i - 1, 0), 0))
    return pl.pallas_call(
        functools.partial(_pool_ffn_kernel, n_tiles=n_tiles, tiles_per_seq=seq // tm,
                          final_norm=final_norm),
        grid=(n_tiles + 1,),
        in_specs=[row_in, halo, _const_spec((1, D_MODEL)),
                  _const_spec((len(POOL_WINDOWS), POOL_GROUP, POOL_GROUP)),
                  _const_spec((1, D_MODEL))] + _ffn_specs(layer),
        out_specs=row_out,
        out_shape=jax.ShapeDtypeStruct((n_tok, D_MODEL), F32),
        scratch_shapes=[pltpu.VMEM((tm + POOL_HALO, D_MODEL), F32)] + _pipeline_scratch(tm),
        compiler_params=_params(1),
        name="pool_ffn",
    )(x, x, g, pw, ps, gf, wg, wu, wdn, gfin)


def _wo_ffn(x, o_attn, wo, gf, wg, wu, wdn, gfin, layer, final_norm):
    n_tok = x.shape[0]
    tm = TOKEN_TILE
    n_tiles = n_tok // tm
    row_in = pl.BlockSpec((tm, D_MODEL), lambda i: (jnp.minimum(i, n_tiles - 1), 0))
    row_out = pl.BlockSpec((tm, D_MODEL), lambda i: (jnp.maximum(i - 1, 0), 0))
    return pl.pallas_call(
        functools.partial(_wo_ffn_kernel, final_norm=final_norm),
        grid=(n_tiles + 1,),
        in_specs=[row_in, row_in, _const_spec((D_MODEL, D_MODEL))] + _ffn_specs(layer),
        out_specs=row_out,
        out_shape=jax.ShapeDtypeStruct((n_tok, D_MODEL), F32),
        scratch_shapes=_pipeline_scratch(tm),
        compiler_params=_params(1),
        name="wo_ffn",
    )(x, o_attn, wo, gf, wg, wu, wdn, gfin)


def kernel(x, positions, norm_mix, norm_ffn, norm_final, pool_w, pool_scale, mla_w_down,
           mla_q_norm, mla_w_uq, mla_kv_norm, mla_w_ukv, mla_w_o, ffn_w_gate, ffn_w_up,
           ffn_w_down):
    batch, seq, d = x.shape
    n_tok = batch * seq
    xt = x.reshape(n_tok, d)
    cos, sin = _rope_tables(positions.reshape(n_tok))
    gfin = norm_final.reshape(1, d)
    n_mixers = 2
    wg = ffn_w_gate.astype(BF16)
    wu = ffn_w_up.astype(BF16)
    wdn = ffn_w_down.astype(BF16)
    for i in range(DEPTH):
        j = i // n_mixers
        final = i == DEPTH - 1
        gm = norm_mix[i].reshape(1, d)
        gf = norm_ffn[i].reshape(1, d)
        if i % n_mixers == 0:
            xt = _pool_ffn(xt, seq, gm, pool_w[j].astype(BF16), pool_scale[j].reshape(1, d),
                           gf, wg, wu, wdn, gfin, i, final)
        else:
            rope_pad = LANES - QK_ROPE_DIM
            wd = jnp.pad(mla_w_down[j], ((0, 0), (0, rope_pad))).astype(BF16)
            wq = mla_w_uq[j].reshape(Q_LORA_RANK, N_HEADS, QK_HEAD_DIM)
            wqn = wq[:, :, :QK_NOPE_DIM].reshape(Q_LORA_RANK, -1).astype(BF16)
            wqr = jnp.pad(wq[:, :, QK_NOPE_DIM:], ((0, 0), (0, 0), (0, rope_pad)))
            wqr = wqr.reshape(Q_LORA_RANK, -1).astype(BF16)
            wkv = mla_w_ukv[j].reshape(KV_LORA_RANK, N_HEADS, QK_NOPE_DIM + V_HEAD_DIM)
            wkn = wkv[:, :, :QK_NOPE_DIM].reshape(KV_LORA_RANK, -1).astype(BF16)
            wv = wkv[:, :, QK_NOPE_DIM:].reshape(KV_LORA_RANK, -1).astype(BF16)
            qn, qr, kn, kr, v = _mla_proj(
                xt, gm, wd, mla_q_norm[j].reshape(1, -1), mla_kv_norm[j].reshape(1, -1),
                wqn, wqr, wkn, wv, cos, sin)
            o_attn = _attention(qn, qr, kn, kr, v, batch, seq)
            xt = _wo_ffn(xt, o_attn, mla_w_o[j].astype(BF16), gf, wg, wu, wdn, gfin, i, final)
    return xt.reshape(batch, seq, d)
```

```python
import functools

import numpy as np
import jax
import jax.numpy as jnp
from jax import lax
from jax.experimental import pallas as pl
from jax.experimental.pallas import tpu as pltpu

D_MODEL = 1024
DEPTH = 4
POOL_WINDOWS = (2, 4, 8, 16)
POOL_GROUP = D_MODEL // len(POOL_WINDOWS)
POOL_HALO = 16
POOL_PAD = 8
N_HEADS = 8
QK_NOPE_DIM = 128
QK_ROPE_DIM = 64
V_HEAD_DIM = 128
Q_LORA_RANK = 384
KV_LORA_RANK = 128
QK_HEAD_DIM = QK_NOPE_DIM + QK_ROPE_DIM
ROPE_THETA = 10000.0
D_FF = 2816
RMS_EPS = 1e-6

LANES = 128
MXU_DIM = 256
FF_CHUNK = MXU_DIM
DOWN_PAD = Q_LORA_RANK + KV_LORA_RANK + LANES

TOKEN_TILE = 512
ATTN_TILE = 512
PROJ_TILE = 1024
PROJ_SUB_TILE = 512
ROPE_TILE = 1024
VMEM_LIMIT = 56 * 1024 * 1024

LOG2_E = 1.4426950408889634
BF16 = jnp.bfloat16
F32 = jnp.float32


def _rms(x, g):
    return x * lax.rsqrt(jnp.mean(x * x, axis=-1, keepdims=True) + RMS_EPS) * g


def _dot(a, b):
    return jnp.dot(a, b, preferred_element_type=F32)


def _const_spec(shape):
    return pl.BlockSpec(shape, lambda *_: (0,) * len(shape), pipeline_mode=pl.Buffered(1))


def _params(n_axes):
    return pltpu.CompilerParams(
        dimension_semantics=("arbitrary",) * n_axes,
        vmem_limit_bytes=VMEM_LIMIT)


def _rope_table_kernel(pos_ref, inv_ref, cos_ref, sin_ref):
    ang = pos_ref[...].astype(F32) * inv_ref[...]
    cos_ref[...] = jnp.cos(ang)
    sin_ref[...] = jnp.sin(ang)


def _rope_tables(positions):
    n_tok = positions.shape[0]
    half = QK_ROPE_DIM // 2
    tok_per_row = LANES // half
    n_rows = n_tok // tok_per_row
    inv = 1.0 / (ROPE_THETA ** (np.arange(0, QK_ROPE_DIM, 2, dtype=np.float32) / QK_ROPE_DIM))
    inv_row = np.tile(inv, tok_per_row)[None, :]
    pos_rows = jnp.repeat(positions.reshape(n_rows, tok_per_row), half, axis=1)
    spec = pl.BlockSpec((ROPE_TILE, LANES), lambda i: (i, 0))
    cos, sin = pl.pallas_call(
        _rope_table_kernel,
        grid=(n_rows // ROPE_TILE,),
        in_specs=[spec, _const_spec((1, LANES))],
        out_specs=[spec] * 2,
        out_shape=[jax.ShapeDtypeStruct((n_rows, LANES), F32)] * 2,
        compiler_params=_params(1),
        name="rope_tables",
    )(pos_rows, jnp.asarray(inv_row))
    return cos.reshape(n_tok, half), sin.reshape(n_tok, half)


def _rope(x, cos, sin):
    lane = lax.broadcasted_iota(jnp.int32, x.shape, 1)
    half = QK_ROPE_DIM // 2
    swapped = jnp.where(lane < half,
                        pltpu.roll(x, LANES - half, axis=1),
                        pltpu.roll(x, half, axis=1))
    return x * cos + swapped * sin


def _mla_proj_kernel(x_ref, g_ref, wd_ref, qn_g_ref, kvn_g_ref, wqn_ref, wqr_ref,
                     wkn_ref, wv_ref, cos_ref, sin_ref,
                     qn_ref, qr_ref, kn_ref, kr_ref, v_ref):
    tm = x_ref.shape[0]
    subs = [pl.ds(r, PROJ_SUB_TILE) for r in range(0, tm, PROJ_SUB_TILE)]
    sm_scale = QK_HEAD_DIM ** -0.5 * LOG2_E
    pad = jnp.zeros((PROJ_SUB_TILE, LANES - QK_ROPE_DIM), F32)
    h = [_rms(x_ref[r, :], g_ref[...]).astype(BF16) for r in subs]
    d = [_dot(hh, wd_ref[...]) for hh in h]
    cq = [_rms(dd[:, :Q_LORA_RANK], qn_g_ref[...]).astype(BF16) for dd in d]
    ckv = [_rms(dd[:, Q_LORA_RANK:Q_LORA_RANK + KV_LORA_RANK], kvn_g_ref[...]).astype(BF16)
           for dd in d]
    cos = [jnp.concatenate([cos_ref[r, :], cos_ref[r, :], pad], axis=1) for r in subs]
    sin = [jnp.concatenate([-sin_ref[r, :], sin_ref[r, :], pad], axis=1) for r in subs]
    for i, r in enumerate(subs):
        kr_ref[r, :] = _rope(d[i][:, Q_LORA_RANK + KV_LORA_RANK:], cos[i], sin[i]).astype(BF16)
    for i, r in enumerate(subs):
        qn_ref[r, :] = (_dot(cq[i], wqn_ref[...]) * sm_scale).astype(BF16)
    qr = [_dot(c, wqr_ref[...]) for c in cq]
    for i, r in enumerate(subs):
        for hd in range(N_HEADS):
            sl = slice(hd * LANES, (hd + 1) * LANES)
            qr_ref[r, sl] = (_rope(qr[i][:, sl], cos[i], sin[i]) * sm_scale).astype(BF16)
    for i, r in enumerate(subs):
        kn_ref[r, :] = _dot(ckv[i], wkn_ref[...]).astype(BF16)
    for i, r in enumerate(subs):
        v_ref[r, :] = _dot(ckv[i], wv_ref[...]).astype(BF16)


def _mla_proj(x, g, wd, qn_g, kvn_g, wqn, wqr, wkn, wv, cos, sin):
    n_tok = x.shape[0]
    tm = PROJ_TILE
    hv = N_HEADS * LANES
    row = lambda w: pl.BlockSpec((tm, w), lambda i: (i, 0))
    return pl.pallas_call(
        _mla_proj_kernel,
        grid=(n_tok // tm,),
        in_specs=[row(D_MODEL), _const_spec((1, D_MODEL)), _const_spec((D_MODEL, DOWN_PAD)),
                  _const_spec((1, Q_LORA_RANK)), _const_spec((1, KV_LORA_RANK)),
                  _const_spec((Q_LORA_RANK, hv)), _const_spec((Q_LORA_RANK, hv)),
                  _const_spec((KV_LORA_RANK, hv)), _const_spec((KV_LORA_RANK, hv)),
                  row(QK_ROPE_DIM // 2), row(QK_ROPE_DIM // 2)],
        out_specs=[row(hv), row(hv), row(hv), row(LANES), row(hv)],
        out_shape=[jax.ShapeDtypeStruct((n_tok, hv), BF16),
                   jax.ShapeDtypeStruct((n_tok, hv), BF16),
                   jax.ShapeDtypeStruct((n_tok, hv), BF16),
                   jax.ShapeDtypeStruct((n_tok, LANES), BF16),
                   jax.ShapeDtypeStruct((n_tok, hv), BF16)],
        compiler_params=_params(1),
        name="mla_proj",
    )(x, g, wd, qn_g, kvn_g, wqn, wqr, wkn, wv, cos, sin)


def _attn_kernel(qn_ref, qr_ref, kn_ref, kr_ref, v_ref, o_ref, m_ref, acc_ref, s_ref):
    qi = pl.program_id(1)
    t = ATTN_TILE
    m_ref[...] = jnp.full(m_ref.shape, -jnp.inf, F32)
    acc_ref[...] = jnp.zeros(acc_ref.shape, F32)
    ones = jnp.ones((t, LANES), BF16)

    def tile_rows(ki):
        return pl.ds(pl.multiple_of(ki * t, t), t)

    def scores(hd, ki):
        sl = slice(hd * LANES, (hd + 1) * LANES)
        rows = tile_rows(ki)
        q = jnp.concatenate([qn_ref[:, sl], qr_ref[:, sl]], axis=1)
        k = jnp.concatenate([kn_ref[rows, sl], kr_ref[rows, :]], axis=1)
        return lax.dot_general(q, k, (((1,), (1,)), ((), ())),
                               preferred_element_type=F32)

    def update(hd, ki, s, masked):
        sl = slice(hd * LANES, (hd + 1) * LANES)
        if masked:
            row = lax.broadcasted_iota(jnp.int32, (t, t), 0)
            col = lax.broadcasted_iota(jnp.int32, (t, t), 1)
            s = jnp.where(col <= row, s, -jnp.inf)
        m_prev = m_ref[hd]
        m_new = jnp.maximum(m_prev, jnp.max(s, axis=1, keepdims=True))
        alpha = jnp.exp2(m_prev - m_new)
        p = jnp.exp2(s - jnp.concatenate([m_new] * (t // LANES), axis=1))
        v_ext = jnp.concatenate([v_ref[tile_rows(ki), sl], ones], axis=1)
        acc_ref[hd] = (jnp.concatenate([alpha, alpha], axis=1) * acc_ref[hd]
                       + _dot(p.astype(BF16), v_ext))
        m_ref[hd] = m_new

    def sweep(ki, masked):
        s = s_ref[...]
        for hd in range(N_HEADS):
            s_next = None
            if hd + 1 < N_HEADS:
                s_next = scores(hd + 1, ki)
            elif not masked:
                s_next = scores(0, ki + 1)
            update(hd, ki, s, masked)
            s = s_next
        if not masked:
            s_ref[...] = s

    def body(ki, carry):
        sweep(ki, False)
        return carry

    s_ref[...] = scores(0, 0)
    lax.fori_loop(0, qi, body, 0)
    sweep(qi, True)
    for hd in range(N_HEADS):
        acc = acc_ref[hd]
        o_ref[:, hd * LANES:(hd + 1) * LANES] = (acc[:, :LANES] / acc[:, LANES:]).astype(BF16)


def _attention(qn, qr, kn, kr, v, batch, seq):
    t = ATTN_TILE
    nq = seq // t
    hv = N_HEADS * LANES
    q_spec = pl.BlockSpec((t, hv), lambda b, i: (b * nq + i, 0))
    seq_spec = lambda w: pl.BlockSpec((seq, w), lambda b, i: (b, 0))
    return pl.pallas_call(
        _attn_kernel,
        grid=(batch, nq),
        in_specs=[q_spec, q_spec, seq_spec(hv), seq_spec(LANES), seq_spec(hv)],
        out_specs=q_spec,
        out_shape=jax.ShapeDtypeStruct((batch * seq, hv), BF16),
        scratch_shapes=[pltpu.VMEM((N_HEADS, t, LANES), F32),
                        pltpu.VMEM((N_HEADS, t, 2 * LANES), F32),
                        pltpu.VMEM((t, t), F32)],
        compiler_params=_params(2),
        name="mla_attention",
    )(qn, qr, kn, kr, v)


def _ffn_stage(x1_ref, h_ref, wg_ref, wu_ref, wdn_ref, gfin_ref, a_ref, o_ref, final_norm,
               side_work):
    h = h_ref[...]
    side_work = list(side_work)
    for c in range(D_FF // FF_CHUNK):
        sl = slice(c * FF_CHUNK, (c + 1) * FF_CHUNK)
        gate = _dot(h, wg_ref[:, sl])
        up = _dot(h, wu_ref[:, sl])
        a_ref[:, sl] = (gate / (1.0 + jnp.exp(-gate)) * up).astype(BF16)
        if side_work:
            side_work.pop(0)()
    for piece in side_work:
        piece()
    out = x1_ref[...] + _dot(a_ref[...], wdn_ref[...])
    if final_norm:
        out = _rms(out, gfin_ref[...])
    o_ref[...] = out


def _pipelined_step(mixer_pieces, gf_ref, ffn_refs, o_ref, x1_slots, h_slots, final_norm):
    i = pl.program_id(0)

    def mixer_and_norm(cur):
        def ffn_norm():
            h_slots[cur][...] = _rms(x1_slots[cur][...], gf_ref[...]).astype(BF16)

        return mixer_pieces(x1_slots[cur]) + [ffn_norm]

    @pl.when(i == 0)
    def _():
        for piece in mixer_and_norm(0):
            piece()

    for cur in (0, 1):
        @pl.when((i > 0) & (i % 2 == cur))
        def _():
            _ffn_stage(x1_slots[1 - cur], h_slots[1 - cur], *ffn_refs, o_ref, final_norm,
                       mixer_and_norm(cur))


def _pool_ffn_kernel(x_ref, halo_ref, g_ref, pw_ref, ps_ref, gf_ref, wg_ref, wu_ref, wdn_ref,
                     gfin_ref, o_ref, ext_ref, st_a, st_b, a_ref, x1_a, x1_b, h_a, h_b,
                     *, n_tiles, tiles_per_seq, final_norm):
    tm = x_ref.shape[0]
    blk = jnp.minimum(pl.program_id(0), n_tiles - 1) % tiles_per_seq
    top = POOL_PAD + POOL_HALO
    n_ext = POOL_HALO + tm

    def mixer_pieces(x1_ref):
        def norm():
            for ref in (ext_ref, st_a, st_b):
                ref[:POOL_PAD, :] = jnp.zeros((POOL_PAD, ref.shape[1]), F32)
            ext_ref[top:, :] = _rms(x_ref[...], g_ref[...])
            halo = _rms(halo_ref[...], g_ref[...])
            ext_ref[POOL_PAD:top, :] = jnp.where(blk == 0, 0.0, halo)

        def group(g, win):
            sl = slice(g * POOL_GROUP, (g + 1) * POOL_GROUP)
            acc = ext_ref[POOL_PAD:, sl] + ext_ref[POOL_PAD - 1:POOL_PAD - 1 + n_ext, sl]
            width, stage = 2, 0
            while width < win:
                buf = (st_a, st_b)[stage % 2]
                buf[POOL_PAD:, :] = acc
                acc = buf[POOL_PAD:, :] + buf[POOL_PAD - width:POOL_PAD - width + n_ext, :]
                width, stage = 2 * width, stage + 1
            pos = blk * tm + lax.broadcasted_iota(jnp.int32, (tm, 1), 0)
            avail = (pos + 1).astype(F32)
            p = acc[POOL_HALO:] / jnp.minimum(avail, float(win)) - ext_ref[top:, sl]
            y = _dot(p.astype(BF16), pw_ref[g])
            x1_ref[:, sl] = x_ref[:, sl] + y * ps_ref[:, sl]

        return [norm] + [functools.partial(group, g, win)
                         for g, win in enumerate(POOL_WINDOWS)]

    _pipelined_step(mixer_pieces, gf_ref, (wg_ref, wu_ref, wdn_ref, gfin_ref, a_ref), o_ref,
                    (x1_a, x1_b), (h_a, h_b), final_norm)


def _wo_ffn_kernel(x_ref, o_attn_ref, wo_ref, gf_ref, wg_ref, wu_ref, wdn_ref,
                   gfin_ref, o_ref, a_ref, x1_a, x1_b, h_a, h_b, *, final_norm):
    def mixer_pieces(x1_ref):
        def out_proj():
            x1_ref[...] = x_ref[...] + _dot(o_attn_ref[...], wo_ref[...])

        return [out_proj]

    _pipelined_step(mixer_pieces, gf_ref, (wg_ref, wu_ref, wdn_ref, gfin_ref, a_ref), o_ref,
                    (x1_a, x1_b), (h_a, h_b), final_norm)


def _layer_spec(layer, shape):
    return pl.BlockSpec((None,) + shape, lambda *_: (layer,) + (0,) * len(shape),
                        pipeline_mode=pl.Buffered(1))


def _ffn_specs(layer):
    return [_const_spec((1, D_MODEL)), _layer_spec(layer, (D_MODEL, D_FF)),
            _layer_spec(layer, (D_MODEL, D_FF)), _layer_spec(layer, (D_FF, D_MODEL)),
            _const_spec((1, D_MODEL))]


def _pipeline_scratch(tm):
    return [pltpu.VMEM((tm, D_FF), BF16),
            pltpu.VMEM((tm, D_MODEL), F32), pltpu.VMEM((tm, D_MODEL), F32),
            pltpu.VMEM((tm, D_MODEL), BF16), pltpu.VMEM((tm, D_MODEL), BF16)]


def _pool_ffn(x, seq, g, pw, ps, gf, wg, wu, wdn, gfin, layer, final_norm):
    n_tok = x.shape[0]
    tm = TOKEN_TILE
    n_tiles = n_tok // tm
    halo_per_tile = tm // POOL_HALO
    tile_in = lambda i: jnp.minimum(i, n_tiles - 1)
    row_in = pl.BlockSpec((tm, D_MODEL), lambda i: (tile_in(i), 0))
    halo = pl.BlockSpec((POOL_HALO, D_MODEL),
                        lambda i: (jnp.maximum(tile_in(i) * halo_per_tile - 1, 0), 0))
    row_out = pl.BlockSpec((tm, D_MODEL), lambda i: (jnp.maximum(i - 1, 0), 0))
    return pl.pallas_call(
        functools.partial(_pool_ffn_kernel, n_tiles=n_tiles, tiles_per_seq=seq // tm,
                          final_norm=final_norm),
        grid=(n_tiles + 1,),
        in_specs=[row_in, halo, _const_spec((1, D_MODEL)),
                  _const_spec((len(POOL_WINDOWS), POOL_GROUP, POOL_GROUP)),
                  _const_spec((1, D_MODEL))] + _ffn_specs(layer),
        out_specs=row_out,
        out_shape=jax.ShapeDtypeStruct((n_tok, D_MODEL), F32),
        scratch_shapes=[pltpu.VMEM((POOL_PAD + POOL_HALO + tm, D_MODEL), F32),
                        pltpu.VMEM((POOL_PAD + POOL_HALO + tm, POOL_GROUP), F32),
                        pltpu.VMEM((POOL_PAD + POOL_HALO + tm, POOL_GROUP), F32)]
        + _pipeline_scratch(tm),
        compiler_params=_params(1),
        name="pool_ffn",
    )(x, x, g, pw, ps, gf, wg, wu, wdn, gfin)


def _wo_ffn(x, o_attn, wo, gf, wg, wu, wdn, gfin, layer, final_norm):
    n_tok = x.shape[0]
    tm = TOKEN_TILE
    n_tiles = n_tok // tm
    row_in = pl.BlockSpec((tm, D_MODEL), lambda i: (jnp.minimum(i, n_tiles - 1), 0))
    row_out = pl.BlockSpec((tm, D_MODEL), lambda i: (jnp.maximum(i - 1, 0), 0))
    return pl.pallas_call(
        functools.partial(_wo_ffn_kernel, final_norm=final_norm),
        grid=(n_tiles + 1,),
        in_specs=[row_in, row_in, _const_spec((D_MODEL, D_MODEL))] + _ffn_specs(layer),
        out_specs=row_out,
        out_shape=jax.ShapeDtypeStruct((n_tok, D_MODEL), F32),
        scratch_shapes=_pipeline_scratch(tm),
        compiler_params=_params(1),
        name="wo_ffn",
    )(x, o_attn, wo, gf, wg, wu, wdn, gfin)


def kernel(x, positions, norm_mix, norm_ffn, norm_final, pool_w, pool_scale, mla_w_down,
           mla_q_norm, mla_w_uq, mla_kv_norm, mla_w_ukv, mla_w_o, ffn_w_gate, ffn_w_up,
           ffn_w_down):
    batch, seq, d = x.shape
    n_tok = batch * seq
    xt = x.reshape(n_tok, d)
    cos, sin = _rope_tables(positions.reshape(n_tok))
    gfin = norm_final.reshape(1, d)
    n_mixers = 2
    wg = ffn_w_gate.astype(BF16)
    wu = ffn_w_up.astype(BF16)
    wdn = ffn_w_down.astype(BF16)
    for i in range(DEPTH):
        j = i // n_mixers
        final = i == DEPTH - 1
        gm = norm_mix[i].reshape(1, d)
        gf = norm_ffn[i].reshape(1, d)
        if i % n_mixers == 0:
            xt = _pool_ffn(xt, seq, gm, pool_w[j].astype(BF16), pool_scale[j].reshape(1, d),
                           gf, wg, wu, wdn, gfin, i, final)
        else:
            rope_pad = LANES - QK_ROPE_DIM
            wd = jnp.pad(mla_w_down[j], ((0, 0), (0, rope_pad))).astype(BF16)
            wq = mla_w_uq[j].reshape(Q_LORA_RANK, N_HEADS, QK_HEAD_DIM)
            wqn = wq[:, :, :QK_NOPE_DIM].reshape(Q_LORA_RANK, -1).astype(BF16)
            wqr = jnp.pad(wq[:, :, QK_NOPE_DIM:], ((0, 0), (0, 0), (0, rope_pad)))
            wqr = wqr.reshape(Q_LORA_RANK, -1).astype(BF16)
            wkv = mla_w_ukv[j].reshape(KV_LORA_RANK, N_HEADS, QK_NOPE_DIM + V_HEAD_DIM)
            wkn = wkv[:, :, :QK_NOPE_DIM].reshape(KV_LORA_RANK, -1).astype(BF16)
            wv = wkv[:, :, QK_NOPE_DIM:].reshape(KV_LORA_RANK, -1).astype(BF16)
            qn, qr, kn, kr, v = _mla_proj(
                xt, gm, wd, mla_q_norm[j].reshape(1, -1), mla_kv_norm[j].reshape(1, -1),
                wqn, wqr, wkn, wv, cos, sin)
            o_attn = _attention(qn, qr, kn, kr, v, batch, seq)
            xt = _wo_ffn(xt, o_attn, mla_w_o[j].astype(BF16), gf, wg, wu, wdn, gfin, i, final)
    return xt.reshape(batch, seq, d)
```

```python
import functools

import numpy as np
import jax
import jax.numpy as jnp
from jax import lax
from jax.experimental import pallas as pl
from jax.experimental.pallas import tpu as pltpu

D_MODEL = 1024
DEPTH = 4
POOL_WINDOWS = (2, 4, 8, 16)
POOL_GROUP = D_MODEL // len(POOL_WINDOWS)
POOL_HALO = 16
POOL_PAD = 8
N_HEADS = 8
QK_NOPE_DIM = 128
QK_ROPE_DIM = 64
V_HEAD_DIM = 128
Q_LORA_RANK = 384
KV_LORA_RANK = 128
QK_HEAD_DIM = QK_NOPE_DIM + QK_ROPE_DIM
ROPE_THETA = 10000.0
D_FF = 2816
RMS_EPS = 1e-6

LANES = 128
MXU_DIM = 256
FF_CHUNK = MXU_DIM
DOWN_PAD = Q_LORA_RANK + KV_LORA_RANK + LANES

TOKEN_TILE = 512
ATTN_TILE = 512
PROJ_TILE = 1024
PROJ_SUB_TILE = 512
ROPE_TILE = 1024
VMEM_LIMIT = 56 * 1024 * 1024

LOG2_E = 1.4426950408889634
BF16 = jnp.bfloat16
F32 = jnp.float32


def _rms(x, g):
    return x * lax.rsqrt(jnp.mean(x * x, axis=-1, keepdims=True) + RMS_EPS) * g


def _dot(a, b):
    return jnp.dot(a, b, preferred_element_type=F32)


def _const_spec(shape):
    return pl.BlockSpec(shape, lambda *_: (0,) * len(shape), pipeline_mode=pl.Buffered(1))


def _params(n_axes):
    return pltpu.CompilerParams(
        dimension_semantics=("arbitrary",) * n_axes,
        vmem_limit_bytes=VMEM_LIMIT)


def _rope_table_kernel(pos_ref, inv_ref, cos_ref, sin_ref):
    ang = pos_ref[...].astype(F32) * inv_ref[...]
    cos_ref[...] = jnp.cos(ang)
    sin_ref[...] = jnp.sin(ang)


def _rope_tables(positions):
    n_tok = positions.shape[0]
    half = QK_ROPE_DIM // 2
    tok_per_row = LANES // half
    n_rows = n_tok // tok_per_row
    inv = 1.0 / (ROPE_THETA ** (np.arange(0, QK_ROPE_DIM, 2, dtype=np.float32) / QK_ROPE_DIM))
    inv_row = np.tile(inv, tok_per_row)[None, :]
    pos_rows = jnp.repeat(positions.reshape(n_rows, tok_per_row), half, axis=1)
    spec = pl.BlockSpec((ROPE_TILE, LANES), lambda i: (i, 0))
    cos, sin = pl.pallas_call(
        _rope_table_kernel,
        grid=(n_rows // ROPE_TILE,),
        in_specs=[spec, _const_spec((1, LANES))],
        out_specs=[spec] * 2,
        out_shape=[jax.ShapeDtypeStruct((n_rows, LANES), F32)] * 2,
        compiler_params=_params(1),
        name="rope_tables",
    )(pos_rows, jnp.asarray(inv_row))
    return cos.reshape(n_tok, half), sin.reshape(n_tok, half)


def _rope(x, cos, sin):
    lane = lax.broadcasted_iota(jnp.int32, x.shape, 1)
    half = QK_ROPE_DIM // 2
    swapped = jnp.where(lane < half,
                        pltpu.roll(x, LANES - half, axis=1),
                        pltpu.roll(x, half, axis=1))
    return x * cos + swapped * sin


def _mla_proj_kernel(x_ref, g_ref, wd_ref, qn_g_ref, kvn_g_ref, wqn_ref, wqr_ref,
                     wkn_ref, wv_ref, cos_ref, sin_ref,
                     qn_ref, qr_ref, kn_ref, kr_ref, v_ref):
    tm = x_ref.shape[0]
    subs = [pl.ds(r, PROJ_SUB_TILE) for r in range(0, tm, PROJ_SUB_TILE)]
    sm_scale = QK_HEAD_DIM ** -0.5 * LOG2_E
    pad = jnp.zeros((PROJ_SUB_TILE, LANES - QK_ROPE_DIM), F32)
    h = [_rms(x_ref[r, :], g_ref[...]).astype(BF16) for r in subs]
    d = [_dot(hh, wd_ref[...]) for hh in h]
    cq = [_rms(dd[:, :Q_LORA_RANK], qn_g_ref[...]).astype(BF16) for dd in d]
    ckv = [_rms(dd[:, Q_LORA_RANK:Q_LORA_RANK + KV_LORA_RANK], kvn_g_ref[...]).astype(BF16)
           for dd in d]
    cos = [jnp.concatenate([cos_ref[r, :], cos_ref[r, :], pad], axis=1) for r in subs]
    sin = [jnp.concatenate([-sin_ref[r, :], sin_ref[r, :], pad], axis=1) for r in subs]
    for i, r in enumerate(subs):
        kr_ref[r, :] = _rope(d[i][:, Q_LORA_RANK + KV_LORA_RANK:], cos[i], sin[i]).astype(BF16)
    for i, r in enumerate(subs):
        qn_ref[r, :] = (_dot(cq[i], wqn_ref[...]) * sm_scale).astype(BF16)
    qr = [_dot(c, wqr_ref[...]) for c in cq]
    for i, r in enumerate(subs):
        for hd in range(N_HEADS):
            sl = slice(hd * LANES, (hd + 1) * LANES)
            qr_ref[r, sl] = (_rope(qr[i][:, sl], cos[i], sin[i]) * sm_scale).astype(BF16)
    for i, r in enumerate(subs):
        kn_ref[r, :] = _dot(ckv[i], wkn_ref[...]).astype(BF16)
    for i, r in enumerate(subs):
        v_ref[r, :] = _dot(ckv[i], wv_ref[...]).astype(BF16)


def _mla_proj(x, g, wd, qn_g, kvn_g, wqn, wqr, wkn, wv, cos, sin):
    n_tok = x.shape[0]
    tm = PROJ_TILE
    hv = N_HEADS * LANES
    row = lambda w: pl.BlockSpec((tm, w), lambda i: (i, 0))
    return pl.pallas_call(
        _mla_proj_kernel,
        grid=(n_tok // tm,),
        in_specs=[row(D_MODEL), _const_spec((1, D_MODEL)), _const_spec((D_MODEL, DOWN_PAD)),
                  _const_spec((1, Q_LORA_RANK)), _const_spec((1, KV_LORA_RANK)),
                  _const_spec((Q_LORA_RANK, hv)), _const_spec((Q_LORA_RANK, hv)),
                  _const_spec((KV_LORA_RANK, hv)), _const_spec((KV_LORA_RANK, hv)),
                  row(QK_ROPE_DIM // 2), row(QK_ROPE_DIM // 2)],
        out_specs=[row(hv), row(hv), row(hv), row(LANES), row(hv)],
        out_shape=[jax.ShapeDtypeStruct((n_tok, hv), BF16),
                   jax.ShapeDtypeStruct((n_tok, hv), BF16),
                   jax.ShapeDtypeStruct((n_tok, hv), BF16),
                   jax.ShapeDtypeStruct((n_tok, LANES), BF16),
                   jax.ShapeDtypeStruct((n_tok, hv), BF16)],
        compiler_params=_params(1),
        name="mla_proj",
    )(x, g, wd, qn_g, kvn_g, wqn, wqr, wkn, wv, cos, sin)


def _attn_kernel(qn_ref, qr_ref, kn_ref, kr_ref, v_ref, o_ref, m_ref, acc_ref, s_ref):
    qi = pl.program_id(1)
    t = ATTN_TILE
    m_ref[...] = jnp.full(m_ref.shape, -jnp.inf, F32)
    acc_ref[...] = jnp.zeros(acc_ref.shape, F32)
    ones = jnp.ones((2 * t, LANES), BF16)

    def key_rows(start, n):
        return pl.ds(pl.multiple_of(start, t), n)

    def scores(hd, start, n):
        sl = slice(hd * LANES, (hd + 1) * LANES)
        rows = key_rows(start, n)
        q = jnp.concatenate([qn_ref[:, sl], qr_ref[:, sl]], axis=1)
        k = jnp.concatenate([kn_ref[rows, sl], kr_ref[rows, :]], axis=1)
        return lax.dot_general(q, k, (((1,), (1,)), ((), ())),
                               preferred_element_type=F32)

    def update(hd, start, s, diag_col=None):
        sl = slice(hd * LANES, (hd + 1) * LANES)
        n = s.shape[1]
        if diag_col is not None:
            row = lax.broadcasted_iota(jnp.int32, s.shape, 0)
            col = lax.broadcasted_iota(jnp.int32, s.shape, 1)
            s = jnp.where(col - diag_col <= row, s, -jnp.inf)
        m_prev = m_ref[hd]
        m_new = jnp.maximum(m_prev, jnp.max(s, axis=1, keepdims=True))
        alpha = jnp.exp2(m_prev - m_new)
        p = jnp.exp2(s - jnp.concatenate([m_new] * (n // LANES), axis=1))
        v_ext = jnp.concatenate([v_ref[key_rows(start, n), sl], ones[:n]], axis=1)
        acc_ref[hd] = (jnp.concatenate([alpha, alpha], axis=1) * acc_ref[hd]
                       + _dot(p.astype(BF16), v_ext))
        m_ref[hd] = m_new

    def sweep(start, n, diag_col, carry_next):
        s = s_ref[...]
        if n > t:
            s = jnp.concatenate([s, scores(0, start + t, n - t)], axis=1)
        for hd in range(N_HEADS):
            s_next = None
            if hd + 1 < N_HEADS:
                s_next = scores(hd + 1, start, n)
            elif carry_next:
                s_next = scores(0, start + n, t)
            update(hd, start, s, diag_col)
            s = s_next
        if carry_next:
            s_ref[...] = s

    def body(kp, carry):
        sweep(kp * 2 * t, 2 * t, None, True)
        return carry

    s_ref[...] = scores(0, 0, t)
    n_pairs = qi // 2
    lax.fori_loop(0, n_pairs, body, 0)

    @pl.when(qi % 2 == 1)
    def _():
        sweep(n_pairs * 2 * t, 2 * t, t, False)

    @pl.when(qi % 2 == 0)
    def _():
        sweep(qi * t, t, 0, False)

    for hd in range(N_HEADS):
        acc = acc_ref[hd]
        o_ref[:, hd * LANES:(hd + 1) * LANES] = (acc[:, :LANES] / acc[:, LANES:]).astype(BF16)


def _attention(qn, qr, kn, kr, v, batch, seq):
    t = ATTN_TILE
    nq = seq // t
    hv = N_HEADS * LANES
    q_spec = pl.BlockSpec((t, hv), lambda b, i: (b * nq + i, 0))
    seq_spec = lambda w: pl.BlockSpec((seq, w), lambda b, i: (b, 0))
    return pl.pallas_call(
        _attn_kernel,
        grid=(batch, nq),
        in_specs=[q_spec, q_spec, seq_spec(hv), seq_spec(LANES), seq_spec(hv)],
        out_specs=q_spec,
        out_shape=jax.ShapeDtypeStruct((batch * seq, hv), BF16),
        scratch_shapes=[pltpu.VMEM((N_HEADS, t, LANES), F32),
                        pltpu.VMEM((N_HEADS, t, 2 * LANES), F32),
                        pltpu.VMEM((t, t), F32)],
        compiler_params=_params(2),
        name="mla_attention",
    )(qn, qr, kn, kr, v)


def _ffn_stage(x1_ref, h_ref, wg_ref, wu_ref, wdn_ref, gfin_ref, a_ref, o_ref, final_norm,
               side_work):
    h = h_ref[...]
    side_work = list(side_work)
    for c in range(D_FF // FF_CHUNK):
        sl = slice(c * FF_CHUNK, (c + 1) * FF_CHUNK)
        gate = _dot(h, wg_ref[:, sl])
        up = _dot(h, wu_ref[:, sl])
        a_ref[:, sl] = (gate / (1.0 + jnp.exp(-gate)) * up).astype(BF16)
        if side_work:
            side_work.pop(0)()
    for piece in side_work:
        piece()
    out = x1_ref[...] + _dot(a_ref[...], wdn_ref[...])
    if final_norm:
        out = _rms(out, gfin_ref[...])
    o_ref[...] = out


def _pipelined_step(mixer_pieces, gf_ref, ffn_refs, o_ref, x1_slots, h_slots, final_norm):
    i = pl.program_id(0)

    def mixer_and_norm(cur):
        def ffn_norm():
            h_slots[cur][...] = _rms(x1_slots[cur][...], gf_ref[...]).astype(BF16)

        return mixer_pieces(x1_slots[cur]) + [ffn_norm]

    @pl.when(i == 0)
    def _():
        for piece in mixer_and_norm(0):
            piece()

    for cur in (0, 1):
        @pl.when((i > 0) & (i % 2 == cur))
        def _():
            _ffn_stage(x1_slots[1 - cur], h_slots[1 - cur], *ffn_refs, o_ref, final_norm,
                       mixer_and_norm(cur))


def _pool_ffn_kernel(x_ref, halo_ref, g_ref, pw_ref, ps_ref, gf_ref, wg_ref, wu_ref, wdn_ref,
                     gfin_ref, o_ref, ext_ref, st_a, st_b, a_ref, x1_a, x1_b, h_a, h_b,
                     *, n_tiles, tiles_per_seq, final_norm):
    tm = x_ref.shape[0]
    blk = jnp.minimum(pl.program_id(0), n_tiles - 1) % tiles_per_seq
    top = POOL_PAD + POOL_HALO
    n_ext = POOL_HALO + tm

    def mixer_pieces(x1_ref):
        def norm():
            for ref in (ext_ref, st_a, st_b):
                ref[:POOL_PAD, :] = jnp.zeros((POOL_PAD, ref.shape[1]), F32)
            ext_ref[top:, :] = _rms(x_ref[...], g_ref[...])
            halo = _rms(halo_ref[...], g_ref[...])
            ext_ref[POOL_PAD:top, :] = jnp.where(blk == 0, 0.0, halo)

        def group(g, win):
            sl = slice(g * POOL_GROUP, (g + 1) * POOL_GROUP)
            acc = ext_ref[POOL_PAD:, sl] + ext_ref[POOL_PAD - 1:POOL_PAD - 1 + n_ext, sl]
            width, stage = 2, 0
            while width < win:
                buf = (st_a, st_b)[stage % 2]
                buf[POOL_PAD:, :] = acc
                acc = buf[POOL_PAD:, :] + buf[POOL_PAD - width:POOL_PAD - width + n_ext, :]
                width, stage = 2 * width, stage + 1
            pos = blk * tm + lax.broadcasted_iota(jnp.int32, (tm, 1), 0)
            avail = (pos + 1).astype(F32)
            p = acc[POOL_HALO:] / jnp.minimum(avail, float(win)) - ext_ref[top:, sl]
            y = _dot(p.astype(BF16), pw_ref[g])
            x1_ref[:, sl] = x_ref[:, sl] + y * ps_ref[:, sl]

        return [norm] + [functools.partial(group, g, win)
                         for g, win in enumerate(POOL_WINDOWS)]

    _pipelined_step(mixer_pieces, gf_ref, (wg_ref, wu_ref, wdn_ref, gfin_ref, a_ref), o_ref,
                    (x1_a, x1_b), (h_a, h_b), final_norm)


def _wo_ffn_kernel(x_ref, o_attn_ref, wo_ref, gf_ref, wg_ref, wu_ref, wdn_ref,
                   gfin_ref, o_ref, a_ref, x1_a, x1_b, h_a, h_b, *, final_norm):
    def mixer_pieces(x1_ref):
        def out_proj():
            x1_ref[...] = x_ref[...] + _dot(o_attn_ref[...], wo_ref[...])

        return [out_proj]

    _pipelined_step(mixer_pieces, gf_ref, (wg_ref, wu_ref, wdn_ref, gfin_ref, a_ref), o_ref,
                    (x1_a, x1_b), (h_a, h_b), final_norm)


def _layer_spec(layer, shape):
    return pl.BlockSpec((None,) + shape, lambda *_: (layer,) + (0,) * len(shape),
                        pipeline_mode=pl.Buffered(1))


def _ffn_specs(layer):
    return [_const_spec((1, D_MODEL)), _layer_spec(layer, (D_MODEL, D_FF)),
            _layer_spec(layer, (D_MODEL, D_FF)), _layer_spec(layer, (D_FF, D_MODEL)),
            _const_spec((1, D_MODEL))]


def _pipeline_scratch(tm):
    return [pltpu.VMEM((tm, D_FF), BF16),
            pltpu.VMEM((tm, D_MODEL), F32), pltpu.VMEM((tm, D_MODEL), F32),
            pltpu.VMEM((tm, D_MODEL), BF16), pltpu.VMEM((tm, D_MODEL), BF16)]


def _pool_ffn(x, seq, g, pw, ps, gf, wg, wu, wdn, gfin, layer, final_norm):
    n_tok = x.shape[0]
    tm = TOKEN_TILE
    n_tiles = n_tok // tm
    halo_per_tile = tm // POOL_HALO
    tile_in = lambda i: jnp.minimum(i, n_tiles - 1)
    row_in = pl.BlockSpec((tm, D_MODEL), lambda i: (tile_in(i), 0))
    halo = pl.BlockSpec((POOL_HALO, D_MODEL),
                        lambda i: (jnp.maximum(tile_in(i) * halo_per_tile - 1, 0), 0))
    row_out = pl.BlockSpec((tm, D_MODEL), lambda i: (jnp.maximum(i - 1, 0), 0))
    return pl.pallas_call(
        functools.partial(_pool_ffn_kernel, n_tiles=n_tiles, tiles_per_seq=seq // tm,
                          final_norm=final_norm),
        grid=(n_tiles + 1,),
        in_specs=[row_in, halo, _const_spec((1, D_MODEL)),
                  _const_spec((len(POOL_WINDOWS), POOL_GROUP, POOL_GROUP)),
                  _const_spec((1, D_MODEL))] + _ffn_specs(layer),
        out_specs=row_out,
        out_shape=jax.ShapeDtypeStruct((n_tok, D_MODEL), F32),
        scratch_shapes=[pltpu.VMEM((POOL_PAD + POOL_HALO + tm, D_MODEL), F32),
                        pltpu.VMEM((POOL_PAD + POOL_HALO + tm, POOL_GROUP), F32),
                        pltpu.VMEM((POOL_PAD + POOL_HALO + tm, POOL_GROUP), F32)]
        + _pipeline_scratch(tm),
        compiler_params=_params(1),
        name="pool_ffn",
    )(x, x, g, pw, ps, gf, wg, wu, wdn, gfin)


def _wo_ffn(x, o_attn, wo, gf, wg, wu, wdn, gfin, layer, final_norm):
    n_tok = x.shape[0]
    tm = TOKEN_TILE
    n_tiles = n_tok // tm
    row_in = pl.BlockSpec((tm, D_MODEL), lambda i: (jnp.minimum(i, n_tiles - 1), 0))
    row_out = pl.BlockSpec((tm, D_MODEL), lambda i: (jnp.maximum(i - 1, 0), 0))
    return pl.pallas_call(
        functools.partial(_wo_ffn_kernel, final_norm=final_norm),
        grid=(n_tiles + 1,),
        in_specs=[row_in, row_in, _const_spec((D_MODEL, D_MODEL))] + _ffn_specs(layer),
        out_specs=row_out,
        out_shape=jax.ShapeDtypeStruct((n_tok, D_MODEL), F32),
        scratch_shapes=_pipeline_scratch(tm),
        compiler_params=_params(1),
        name="wo_ffn",
    )(x, o_attn, wo, gf, wg, wu, wdn, gfin)


def kernel(x, positions, norm_mix, norm_ffn, norm_final, pool_w, pool_scale, mla_w_down,
           mla_q_norm, mla_w_uq, mla_kv_norm, mla_w_ukv, mla_w_o, ffn_w_gate, ffn_w_up,
           ffn_w_down):
    batch, seq, d = x.shape
    n_tok = batch * seq
    xt = x.reshape(n_tok, d)
    cos, sin = _rope_tables(positions.reshape(n_tok))
    gfin = norm_final.reshape(1, d)
    n_mixers = 2
    wg = ffn_w_gate.astype(BF16)
    wu = ffn_w_up.astype(BF16)
    wdn = ffn_w_down.astype(BF16)
    for i in range(DEPTH):
        j = i // n_mixers
        final = i == DEPTH - 1
        gm = norm_mix[i].reshape(1, d)
        gf = norm_ffn[i].reshape(1, d)
        if i % n_mixers == 0:
            xt = _pool_ffn(xt, seq, gm, pool_w[j].astype(BF16), pool_scale[j].reshape(1, d),
                           gf, wg, wu, wdn, gfin, i, final)
        else:
            rope_pad = LANES - QK_ROPE_DIM
            wd = jnp.pad(mla_w_down[j], ((0, 0), (0, rope_pad))).astype(BF16)
            wq = mla_w_uq[j].reshape(Q_LORA_RANK, N_HEADS, QK_HEAD_DIM)
            wqn = wq[:, :, :QK_NOPE_DIM].reshape(Q_LORA_RANK, -1).astype(BF16)
            wqr = jnp.pad(wq[:, :, QK_NOPE_DIM:], ((0, 0), (0, 0), (0, rope_pad)))
            wqr = wqr.reshape(Q_LORA_RANK, -1).astype(BF16)
            wkv = mla_w_ukv[j].reshape(KV_LORA_RANK, N_HEADS, QK_NOPE_DIM + V_HEAD_DIM)
            wkn = wkv[:, :, :QK_NOPE_DIM].reshape(KV_LORA_RANK, -1).astype(BF16)
            wv = wkv[:, :, QK_NOPE_DIM:].reshape(KV_LORA_RANK, -1).astype(BF16)
            qn, qr, kn, kr, v = _mla_proj(
                xt, gm, wd, mla_q_norm[j].reshape(1, -1), mla_kv_norm[j].reshape(1, -1),
                wqn, wqr, wkn, wv, cos, sin)
            o_attn = _attention(qn, qr, kn, kr, v, batch, seq)
            xt = _wo_ffn(xt, o_attn, mla_w_o[j].astype(BF16), gf, wg, wu, wdn, gfin, i, final)
    return xt.reshape(batch, seq, d)
```

```python
import functools

import numpy as np
import jax
import jax.numpy as jnp
from jax import lax
from jax.experimental import pallas as pl
from jax.experimental.pallas import tpu as pltpu

D_MODEL = 1024
DEPTH = 4
POOL_WINDOWS = (2, 4, 8, 16)
POOL_GROUP = D_MODEL // len(POOL_WINDOWS)
POOL_HALO = 16
POOL_PAD = 8
N_HEADS = 8
QK_NOPE_DIM = 128
QK_ROPE_DIM = 64
V_HEAD_DIM = 128
Q_LORA_RANK = 384
KV_LORA_RANK = 128
QK_HEAD_DIM = QK_NOPE_DIM + QK_ROPE_DIM
ROPE_THETA = 10000.0
D_FF = 2816
RMS_EPS = 1e-6

LANES = 128
MXU_DIM = 256
FF_CHUNK = MXU_DIM
DOWN_PAD = Q_LORA_RANK + KV_LORA_RANK + LANES

TOKEN_TILE = 512
ATTN_TILE = 512
PROJ_TILE = 1024
PROJ_SUB_TILE = 512
ROPE_TILE = 1024
VMEM_LIMIT = 56 * 1024 * 1024

LOG2_E = 1.4426950408889634
BF16 = jnp.bfloat16
F32 = jnp.float32


def _rms(x, g):
    return x * lax.rsqrt(jnp.mean(x * x, axis=-1, keepdims=True) + RMS_EPS) * g


def _dot(a, b):
    return jnp.dot(a, b, preferred_element_type=F32)


def _const_spec(shape):
    return pl.BlockSpec(shape, lambda *_: (0,) * len(shape), pipeline_mode=pl.Buffered(1))


def _params(n_axes):
    return pltpu.CompilerParams(
        dimension_semantics=("arbitrary",) * n_axes,
        vmem_limit_bytes=VMEM_LIMIT)


def _rope_table_kernel(pos_ref, inv_ref, cos_ref, sin_ref):
    ang = pos_ref[...].astype(F32) * inv_ref[...]
    cos_ref[...] = jnp.cos(ang)
    sin_ref[...] = jnp.sin(ang)


def _rope_tables(positions):
    n_tok = positions.shape[0]
    half = QK_ROPE_DIM // 2
    tok_per_row = LANES // half
    n_rows = n_tok // tok_per_row
    inv = 1.0 / (ROPE_THETA ** (np.arange(0, QK_ROPE_DIM, 2, dtype=np.float32) / QK_ROPE_DIM))
    inv_row = np.tile(inv, tok_per_row)[None, :]
    pos_rows = jnp.repeat(positions.reshape(n_rows, tok_per_row), half, axis=1)
    spec = pl.BlockSpec((ROPE_TILE, LANES), lambda i: (i, 0))
    cos, sin = pl.pallas_call(
        _rope_table_kernel,
        grid=(n_rows // ROPE_TILE,),
        in_specs=[spec, _const_spec((1, LANES))],
        out_specs=[spec] * 2,
        out_shape=[jax.ShapeDtypeStruct((n_rows, LANES), F32)] * 2,
        compiler_params=_params(1),
        name="rope_tables",
    )(pos_rows, jnp.asarray(inv_row))
    return cos.reshape(n_tok, half), sin.reshape(n_tok, half)


def _rope(x, cos, sin):
    lane = lax.broadcasted_iota(jnp.int32, x.shape, 1)
    half = QK_ROPE_DIM // 2
    swapped = jnp.where(lane < half,
                        pltpu.roll(x, LANES - half, axis=1),
                        pltpu.roll(x, half, axis=1))
    return x * cos + swapped * sin


def _mla_proj_kernel(x_ref, g_ref, wd_ref, qn_g_ref, kvn_g_ref, wqn_ref, wqr_ref,
                     wkn_ref, wv_ref, cos_ref, sin_ref,
                     qn_ref, qr_ref, kn_ref, kr_ref, v_ref):
    tm = x_ref.shape[0]
    subs = [pl.ds(r, PROJ_SUB_TILE) for r in range(0, tm, PROJ_SUB_TILE)]
    sm_scale = QK_HEAD_DIM ** -0.5 * LOG2_E
    pad = jnp.zeros((PROJ_SUB_TILE, LANES - QK_ROPE_DIM), F32)
    h = [_rms(x_ref[r, :], g_ref[...]).astype(BF16) for r in subs]
    d = [_dot(hh, wd_ref[...]) for hh in h]
    cq = [_rms(dd[:, :Q_LORA_RANK], qn_g_ref[...]).astype(BF16) for dd in d]
    ckv = [_rms(dd[:, Q_LORA_RANK:Q_LORA_RANK + KV_LORA_RANK], kvn_g_ref[...]).astype(BF16)
           for dd in d]
    cos = [jnp.concatenate([cos_ref[r, :], cos_ref[r, :], pad], axis=1) for r in subs]
    sin = [jnp.concatenate([-sin_ref[r, :], sin_ref[r, :], pad], axis=1) for r in subs]
    for i, r in enumerate(subs):
        kr_ref[r, :] = _rope(d[i][:, Q_LORA_RANK + KV_LORA_RANK:], cos[i], sin[i]).astype(BF16)
    for i, r in enumerate(subs):
        qn_ref[r, :] = (_dot(cq[i], wqn_ref[...]) * sm_scale).astype(BF16)
    qr = [_dot(c, wqr_ref[...]) for c in cq]
    for i, r in enumerate(subs):
        for hd in range(N_HEADS):
            sl = slice(hd * LANES, (hd + 1) * LANES)
            qr_ref[r, sl] = (_rope(qr[i][:, sl], cos[i], sin[i]) * sm_scale).astype(BF16)
    for i, r in enumerate(subs):
        kn_ref[r, :] = _dot(ckv[i], wkn_ref[...]).astype(BF16)
    for i, r in enumerate(subs):
        v_ref[r, :] = _dot(ckv[i], wv_ref[...]).astype(BF16)


def _mla_proj(x, g, wd, qn_g, kvn_g, wqn, wqr, wkn, wv, cos, sin):
    n_tok = x.shape[0]
    tm = PROJ_TILE
    hv = N_HEADS * LANES
    row = lambda w: pl.BlockSpec((tm, w), lambda i: (i, 0))
    return pl.pallas_call(
        _mla_proj_kernel,
        grid=(n_tok // tm,),
        in_specs=[row(D_MODEL), _const_spec((1, D_MODEL)), _const_spec((D_MODEL, DOWN_PAD)),
                  _const_spec((1, Q_LORA_RANK)), _const_spec((1, KV_LORA_RANK)),
                  _const_spec((Q_LORA_RANK, hv)), _const_spec((Q_LORA_RANK, hv)),
                  _const_spec((KV_LORA_RANK, hv)), _const_spec((KV_LORA_RANK, hv)),
                  row(QK_ROPE_DIM // 2), row(QK_ROPE_DIM // 2)],
        out_specs=[row(hv), row(hv), row(hv), row(LANES), row(hv)],
        out_shape=[jax.ShapeDtypeStruct((n_tok, hv), BF16),
                   jax.ShapeDtypeStruct((n_tok, hv), BF16),
                   jax.ShapeDtypeStruct((n_tok, hv), BF16),
                   jax.ShapeDtypeStruct((n_tok, LANES), BF16),
                   jax.ShapeDtypeStruct((n_tok, hv), BF16)],
        compiler_params=_params(1),
        name="mla_proj",
    )(x, g, wd, qn_g, kvn_g, wqn, wqr, wkn, wv, cos, sin)


def _attn_kernel(qn_ref, qr_ref, kn_ref, kr_ref, v_ref, o_ref, m_ref, acc_ref, s_ref):
    qi = pl.program_id(1)
    t = ATTN_TILE
    ones = jnp.ones((2 * t, LANES), BF16)

    def key_rows(start, n):
        return pl.ds(pl.multiple_of(start, t), n)

    def scores(hd, start, n):
        sl = slice(hd * LANES, (hd + 1) * LANES)
        rows = key_rows(start, n)
        q = jnp.concatenate([qn_ref[:, sl], qr_ref[:, sl]], axis=1)
        k = jnp.concatenate([kn_ref[rows, sl], kr_ref[rows, :]], axis=1)
        return lax.dot_general(q, k, (((1,), (1,)), ((), ())),
                               preferred_element_type=F32)

    def update(hd, start, s, diag_col=None, first=False):
        sl = slice(hd * LANES, (hd + 1) * LANES)
        n = s.shape[1]
        if diag_col is not None:
            row = lax.broadcasted_iota(jnp.int32, s.shape, 0)
            col = lax.broadcasted_iota(jnp.int32, s.shape, 1)
            s = jnp.where(col - diag_col <= row, s, -jnp.inf)
        m_new = jnp.broadcast_to(jnp.max(s, axis=1, keepdims=True), (t, LANES))
        if not first:
            m_prev = m_ref[hd]
            m_new = jnp.maximum(m_prev, m_new)
            alpha = jnp.exp2(m_prev - m_new)
        p = jnp.exp2(s - jnp.concatenate([m_new] * (n // LANES), axis=1))
        v_ext = jnp.concatenate([v_ref[key_rows(start, n), sl], ones[:n]], axis=1)
        pv = _dot(p.astype(BF16), v_ext)
        if not first:
            pv = jnp.concatenate([alpha, alpha], axis=1) * acc_ref[hd] + pv
        acc_ref[hd] = pv
        m_ref[hd] = m_new

    def sweep(start, n, diag_col, carry_next, first=False):
        s = s_ref[...]
        if n > t:
            s = jnp.concatenate([s, scores(0, start + t, n - t)], axis=1)
        for hd in range(N_HEADS):
            s_next = None
            if hd + 1 < N_HEADS:
                s_next = scores(hd + 1, start, n)
            elif carry_next:
                s_next = scores(0, start + n, t)
            update(hd, start, s, diag_col, first)
            s = s_next
        if carry_next:
            s_ref[...] = s

    def body(kp, carry):
        sweep(kp * 2 * t, 2 * t, None, True)
        return carry

    s_ref[...] = scores(0, 0, t)
    n_pairs = qi // 2

    @pl.when(n_pairs == 0)
    def _():
        m_ref[...] = jnp.full(m_ref.shape, -jnp.inf, F32)
        acc_ref[...] = jnp.zeros(acc_ref.shape, F32)

    @pl.when(n_pairs > 0)
    def _():
        sweep(0, 2 * t, None, True, first=True)

    lax.fori_loop(1, n_pairs, body, 0)

    @pl.when(qi % 2 == 1)
    def _():
        sweep(n_pairs * 2 * t, 2 * t, t, False)

    @pl.when(qi % 2 == 0)
    def _():
        sweep(qi * t, t, 0, False)

    for hd in range(N_HEADS):
        acc = acc_ref[hd]
        o_ref[:, hd * LANES:(hd + 1) * LANES] = (acc[:, :LANES] / acc[:, LANES:]).astype(BF16)


def _attention(qn, qr, kn, kr, v, batch, seq):
    t = ATTN_TILE
    nq = seq // t
    hv = N_HEADS * LANES
    q_spec = pl.BlockSpec((t, hv), lambda b, i: (b * nq + i, 0))
    seq_spec = lambda w: pl.BlockSpec((seq, w), lambda b, i: (b, 0))
    return pl.pallas_call(
        _attn_kernel,
        grid=(batch, nq),
        in_specs=[q_spec, q_spec, seq_spec(hv), seq_spec(LANES), seq_spec(hv)],
        out_specs=q_spec,
        out_shape=jax.ShapeDtypeStruct((batch * seq, hv), BF16),
        scratch_shapes=[pltpu.VMEM((N_HEADS, t, LANES), F32),
                        pltpu.VMEM((N_HEADS, t, 2 * LANES), F32),
                        pltpu.VMEM((t, t), F32)],
        compiler_params=_params(2),
        name="mla_attention",
    )(qn, qr, kn, kr, v)


def _ffn_stage(x1_ref, h_ref, wg_ref, wu_ref, wdn_ref, gfin_ref, a_ref, o_ref, final_norm,
               side_work):
    h = h_ref[...]
    side_work = list(side_work)
    for c in range(D_FF // FF_CHUNK):
        sl = slice(c * FF_CHUNK, (c + 1) * FF_CHUNK)
        gate = _dot(h, wg_ref[:, sl])
        up = _dot(h, wu_ref[:, sl])
        a_ref[:, sl] = (gate / (1.0 + jnp.exp(-gate)) * up).astype(BF16)
        if side_work:
            side_work.pop(0)()
    for piece in side_work:
        piece()
    out = x1_ref[...] + _dot(a_ref[...], wdn_ref[...])
    if final_norm:
        out = _rms(out, gfin_ref[...])
    o_ref[...] = out


def _pipelined_step(mixer_pieces, gf_ref, ffn_refs, o_ref, x1_slots, h_slots, final_norm):
    i = pl.program_id(0)

    def mixer_and_norm(cur):
        def ffn_norm():
            h_slots[cur][...] = _rms(x1_slots[cur][...], gf_ref[...]).astype(BF16)

        return mixer_pieces(x1_slots[cur]) + [ffn_norm]

    @pl.when(i == 0)
    def _():
        for piece in mixer_and_norm(0):
            piece()

    for cur in (0, 1):
        @pl.when((i > 0) & (i % 2 == cur))
        def _():
            _ffn_stage(x1_slots[1 - cur], h_slots[1 - cur], *ffn_refs, o_ref, final_norm,
                       mixer_and_norm(cur))


def _pool_ffn_kernel(x_ref, halo_ref, g_ref, pw_ref, ps_ref, gf_ref, wg_ref, wu_ref, wdn_ref,
                     gfin_ref, o_ref, ext_ref, st_a, st_b, a_ref, x1_a, x1_b, h_a, h_b,
                     *, n_tiles, tiles_per_seq, final_norm):
    tm = x_ref.shape[0]
    blk = jnp.minimum(pl.program_id(0), n_tiles - 1) % tiles_per_seq
    top = POOL_PAD + POOL_HALO
    n_ext = POOL_HALO + tm

    def mixer_pieces(x1_ref):
        def norm():
            for ref in (ext_ref, st_a, st_b):
                ref[:POOL_PAD, :] = jnp.zeros((POOL_PAD, ref.shape[1]), F32)
            ext_ref[top:, :] = _rms(x_ref[...], g_ref[...])
            halo = _rms(halo_ref[...], g_ref[...])
            ext_ref[POOL_PAD:top, :] = jnp.where(blk == 0, 0.0, halo)

        def group(g, win):
            sl = slice(g * POOL_GROUP, (g + 1) * POOL_GROUP)
            acc = ext_ref[POOL_PAD:, sl] + ext_ref[POOL_PAD - 1:POOL_PAD - 1 + n_ext, sl]
            width, stage = 2, 0
            while width < win:
                buf = (st_a, st_b)[stage % 2]
                buf[POOL_PAD:, :] = acc
                acc = buf[POOL_PAD:, :] + buf[POOL_PAD - width:POOL_PAD - width + n_ext, :]
                width, stage = 2 * width, stage + 1
            pos = blk * tm + lax.broadcasted_iota(jnp.int32, (tm, 1), 0)
            avail = (pos + 1).astype(F32)
            p = acc[POOL_HALO:] / jnp.minimum(avail, float(win)) - ext_ref[top:, sl]
            y = _dot(p.astype(BF16), pw_ref[g])
            x1_ref[:, sl] = x_ref[:, sl] + y * ps_ref[:, sl]

        return [norm] + [functools.partial(group, g, win)
                         for g, win in enumerate(POOL_WINDOWS)]

    _pipelined_step(mixer_pieces, gf_ref, (wg_ref, wu_ref, wdn_ref, gfin_ref, a_ref), o_ref,
                    (x1_a, x1_b), (h_a, h_b), final_norm)


def _wo_ffn_kernel(x_ref, o_attn_ref, wo_ref, gf_ref, wg_ref, wu_ref, wdn_ref,
                   gfin_ref, o_ref, a_ref, x1_a, x1_b, h_a, h_b, *, final_norm):
    def mixer_pieces(x1_ref):
        def out_proj():
            x1_ref[...] = x_ref[...] + _dot(o_attn_ref[...], wo_ref[...])

        return [out_proj]

    _pipelined_step(mixer_pieces, gf_ref, (wg_ref, wu_ref, wdn_ref, gfin_ref, a_ref), o_ref,
                    (x1_a, x1_b), (h_a, h_b), final_norm)


def _layer_spec(layer, shape):
    return pl.BlockSpec((None,) + shape, lambda *_: (layer,) + (0,) * len(shape),
                        pipeline_mode=pl.Buffered(1))


def _ffn_specs(layer):
    return [_const_spec((1, D_MODEL)), _layer_spec(layer, (D_MODEL, D_FF)),
            _layer_spec(layer, (D_MODEL, D_FF)), _layer_spec(layer, (D_FF, D_MODEL)),
            _const_spec((1, D_MODEL))]


def _pipeline_scratch(tm):
    return [pltpu.VMEM((tm, D_FF), BF16),
            pltpu.VMEM((tm, D_MODEL), F32), pltpu.VMEM((tm, D_MODEL), F32),
            pltpu.VMEM((tm, D_MODEL), BF16), pltpu.VMEM((tm, D_MODEL), BF16)]


def _pool_ffn(x, seq, g, pw, ps, gf, wg, wu, wdn, gfin, layer, final_norm):
    n_tok = x.shape[0]
    tm = TOKEN_TILE
    n_tiles = n_tok // tm
    halo_per_tile = tm // POOL_HALO
    tile_in = lambda i: jnp.minimum(i, n_tiles - 1)
    row_in = pl.BlockSpec((tm, D_MODEL), lambda i: (tile_in(i), 0))
    halo = pl.BlockSpec((POOL_HALO, D_MODEL),
                        lambda i: (jnp.maximum(tile_in(i) * halo_per_tile - 1, 0), 0))
    row_out = pl.BlockSpec((tm, D_MODEL), lambda i: (jnp.maximum(i - 1, 0), 0))
    return pl.pallas_call(
        functools.partial(_pool_ffn_kernel, n_tiles=n_tiles, tiles_per_seq=seq // tm,
                          final_norm=final_norm),
        grid=(n_tiles + 1,),
        in_specs=[row_in, halo, _const_spec((1, D_MODEL)),
                  _const_spec((len(POOL_WINDOWS), POOL_GROUP, POOL_GROUP)),
                  _const_spec((1, D_MODEL))] + _ffn_specs(layer),
        out_specs=row_out,
        out_shape=jax.ShapeDtypeStruct((n_tok, D_MODEL), F32),
        scratch_shapes=[pltpu.VMEM((POOL_PAD + POOL_HALO + tm, D_MODEL), F32),
                        pltpu.VMEM((POOL_PAD + POOL_HALO + tm, POOL_GROUP), F32),
                        pltpu.VMEM((POOL_PAD + POOL_HALO + tm, POOL_GROUP), F32)]
        + _pipeline_scratch(tm),
        compiler_params=_params(1),
        name="pool_ffn",
    )(x, x, g, pw, ps, gf, wg, wu, wdn, gfin)


def _wo_ffn(x, o_attn, wo, gf, wg, wu, wdn, gfin, layer, final_norm):
    n_tok = x.shape[0]
    tm = TOKEN_TILE
    n_tiles = n_tok // tm
    row_in = pl.BlockSpec((tm, D_MODEL), lambda i: (jnp.minimum(i, n_tiles - 1), 0))
    row_out = pl.BlockSpec((tm, D_MODEL), lambda i: (jnp.maximum(i - 1, 0), 0))
    return pl.pallas_call(
        functools.partial(_wo_ffn_kernel, final_norm=final_norm),
        grid=(n_tiles + 1,),
        in_specs=[row_in, row_in, _const_spec((D_MODEL, D_MODEL))] + _ffn_specs(layer),
        out_specs=row_out,
        out_shape=jax.ShapeDtypeStruct((n_tok, D_MODEL), F32),
        scratch_shapes=_pipeline_scratch(tm),
        compiler_params=_params(1),
        name="wo_ffn",
    )(x, o_attn, wo, gf, wg, wu, wdn, gfin)


def kernel(x, positions, norm_mix, norm_ffn, norm_final, pool_w, pool_scale, mla_w_down,
           mla_q_norm, mla_w_uq, mla_kv_norm, mla_w_ukv, mla_w_o, ffn_w_gate, ffn_w_up,
           ffn_w_down):
    batch, seq, d = x.shape
    n_tok = batch * seq
    xt = x.reshape(n_tok, d)
    cos, sin = _rope_tables(positions.reshape(n_tok))
    gfin = norm_final.reshape(1, d)
    n_mixers = 2
    wg = ffn_w_gate.astype(BF16)
    wu = ffn_w_up.astype(BF16)
    wdn = ffn_w_down.astype(BF16)
    for i in range(DEPTH):
        j = i // n_mixers
        final = i == DEPTH - 1
        gm = norm_mix[i].reshape(1, d)
        gf = norm_ffn[i].reshape(1, d)
        if i % n_mixers == 0:
            xt = _pool_ffn(xt, seq, gm, pool_w[j].astype(BF16), pool_scale[j].reshape(1, d),
                           gf, wg, wu, wdn, gfin, i, final)
        else:
            rope_pad = LANES - QK_ROPE_DIM
            wd = jnp.pad(mla_w_down[j], ((0, 0), (0, rope_pad))).astype(BF16)
            wq = mla_w_uq[j].reshape(Q_LORA_RANK, N_HEADS, QK_HEAD_DIM)
            wqn = wq[:, :, :QK_NOPE_DIM].reshape(Q_LORA_RANK, -1).astype(BF16)
            wqr = jnp.pad(wq[:, :, QK_NOPE_DIM:], ((0, 0), (0, 0), (0, rope_pad)))
            wqr = wqr.reshape(Q_LORA_RANK, -1).astype(BF16)
            wkv = mla_w_ukv[j].reshape(KV_LORA_RANK, N_HEADS, QK_NOPE_DIM + V_HEAD_DIM)
            wkn = wkv[:, :, :QK_NOPE_DIM].reshape(KV_LORA_RANK, -1).astype(BF16)
            wv = wkv[:, :, QK_NOPE_DIM:].reshape(KV_LORA_RANK, -1).astype(BF16)
            qn, qr, kn, kr, v = _mla_proj(
                xt, gm, wd, mla_q_norm[j].reshape(1, -1), mla_kv_norm[j].reshape(1, -1),
                wqn, wqr, wkn, wv, cos, sin)
            o_attn = _attention(qn, qr, kn, kr, v, batch, seq)
            xt = _wo_ffn(xt, o_attn, mla_w_o[j].astype(BF16), gf, wg, wu, wdn, gfin, i, final)
    return xt.reshape(batch, seq, d)
```

```python
import functools

import numpy as np
import jax
import jax.numpy as jnp
from jax import lax
from jax.experimental import pallas as pl
from jax.experimental.pallas import tpu as pltpu

D_MODEL = 1024
DEPTH = 4
POOL_WINDOWS = (2, 4, 8, 16)
POOL_GROUP = D_MODEL // len(POOL_WINDOWS)
POOL_HALO = 16
POOL_PAD = 8
N_HEADS = 8
QK_NOPE_DIM = 128
QK_ROPE_DIM = 64
V_HEAD_DIM = 128
Q_LORA_RANK = 384
KV_LORA_RANK = 128
QK_HEAD_DIM = QK_NOPE_DIM + QK_ROPE_DIM
ROPE_THETA = 10000.0
D_FF = 2816
RMS_EPS = 1e-6

LANES = 128
MXU_DIM = 256
FF_CHUNK = MXU_DIM
DOWN_PAD = Q_LORA_RANK + KV_LORA_RANK + LANES

TOKEN_TILE = 1024
DOWN_PASS_COLS = 6 * FF_CHUNK
ATTN_TILE = 512
PROJ_TILE = 1024
PROJ_SUB_TILE = 512
ROPE_TILE = 1024
VMEM_LIMIT = 60 * 1024 * 1024

LOG2_E = 1.4426950408889634
BF16 = jnp.bfloat16
F32 = jnp.float32


def _rms(x, g):
    return x * lax.rsqrt(jnp.mean(x * x, axis=-1, keepdims=True) + RMS_EPS) * g


def _dot(a, b):
    return jnp.dot(a, b, preferred_element_type=F32)


def _const_spec(shape):
    return pl.BlockSpec(shape, lambda *_: (0,) * len(shape), pipeline_mode=pl.Buffered(1))


def _params(n_axes):
    return pltpu.CompilerParams(
        dimension_semantics=("arbitrary",) * n_axes,
        vmem_limit_bytes=VMEM_LIMIT)


def _rope_table_kernel(pos_ref, inv_ref, cos_ref, sin_ref):
    ang = pos_ref[...].astype(F32) * inv_ref[...]
    cos_ref[...] = jnp.cos(ang)
    sin_ref[...] = jnp.sin(ang)


def _rope_tables(positions):
    n_tok = positions.shape[0]
    half = QK_ROPE_DIM // 2
    tok_per_row = LANES // half
    n_rows = n_tok // tok_per_row
    inv = 1.0 / (ROPE_THETA ** (np.arange(0, QK_ROPE_DIM, 2, dtype=np.float32) / QK_ROPE_DIM))
    inv_row = np.tile(inv, tok_per_row)[None, :]
    pos_rows = jnp.repeat(positions.reshape(n_rows, tok_per_row), half, axis=1)
    spec = pl.BlockSpec((ROPE_TILE, LANES), lambda i: (i, 0))
    cos, sin = pl.pallas_call(
        _rope_table_kernel,
        grid=(n_rows // ROPE_TILE,),
        in_specs=[spec, _const_spec((1, LANES))],
        out_specs=[spec] * 2,
        out_shape=[jax.ShapeDtypeStruct((n_rows, LANES), F32)] * 2,
        compiler_params=_params(1),
        name="rope_tables",
    )(pos_rows, jnp.asarray(inv_row))
    return cos.reshape(n_tok, half), sin.reshape(n_tok, half)


def _rope(x, cos, sin):
    lane = lax.broadcasted_iota(jnp.int32, x.shape, 1)
    half = QK_ROPE_DIM // 2
    swapped = jnp.where(lane < half,
                        pltpu.roll(x, LANES - half, axis=1),
                        pltpu.roll(x, half, axis=1))
    return x * cos + swapped * sin


def _mla_proj_kernel(x_ref, g_ref, wd_ref, qn_g_ref, kvn_g_ref, wqn_ref, wqr_ref,
                     wkn_ref, wv_ref, cos_ref, sin_ref,
                     qn_ref, qr_ref, kn_ref, kr_ref, v_ref):
    tm = x_ref.shape[0]
    subs = [pl.ds(r, PROJ_SUB_TILE) for r in range(0, tm, PROJ_SUB_TILE)]
    sm_scale = QK_HEAD_DIM ** -0.5 * LOG2_E
    pad = jnp.zeros((PROJ_SUB_TILE, LANES - QK_ROPE_DIM), F32)
    h = [_rms(x_ref[r, :], g_ref[...]).astype(BF16) for r in subs]
    d = [_dot(hh, wd_ref[...]) for hh in h]
    cq = [_rms(dd[:, :Q_LORA_RANK], qn_g_ref[...]).astype(BF16) for dd in d]
    ckv = [_rms(dd[:, Q_LORA_RANK:Q_LORA_RANK + KV_LORA_RANK], kvn_g_ref[...]).astype(BF16)
           for dd in d]
    cos = [jnp.concatenate([cos_ref[r, :], cos_ref[r, :], pad], axis=1) for r in subs]
    sin = [jnp.concatenate([-sin_ref[r, :], sin_ref[r, :], pad], axis=1) for r in subs]
    for i, r in enumerate(subs):
        kr_ref[r, :] = _rope(d[i][:, Q_LORA_RANK + KV_LORA_RANK:], cos[i], sin[i]).astype(BF16)
    for i, r in enumerate(subs):
        qn_ref[r, :] = (_dot(cq[i], wqn_ref[...]) * sm_scale).astype(BF16)
    qr = [_dot(c, wqr_ref[...]) for c in cq]
    for i, r in enumerate(subs):
        for hd in range(N_HEADS):
            sl = slice(hd * LANES, (hd + 1) * LANES)
            qr_ref[r, sl] = (_rope(qr[i][:, sl], cos[i], sin[i]) * sm_scale).astype(BF16)
    for i, r in enumerate(subs):
        kn_ref[r, :] = _dot(ckv[i], wkn_ref[...]).astype(BF16)
    for i, r in enumerate(subs):
        v_ref[r, :] = _dot(ckv[i], wv_ref[...]).astype(BF16)


def _mla_proj(x, g, wd, qn_g, kvn_g, wqn, wqr, wkn, wv, cos, sin):
    n_tok = x.shape[0]
    tm = PROJ_TILE
    hv = N_HEADS * LANES
    row = lambda w: pl.BlockSpec((tm, w), lambda i: (i, 0))
    return pl.pallas_call(
        _mla_proj_kernel,
        grid=(n_tok // tm,),
        in_specs=[row(D_MODEL), _const_spec((1, D_MODEL)), _const_spec((D_MODEL, DOWN_PAD)),
                  _const_spec((1, Q_LORA_RANK)), _const_spec((1, KV_LORA_RANK)),
                  _const_spec((Q_LORA_RANK, hv)), _const_spec((Q_LORA_RANK, hv)),
                  _const_spec((KV_LORA_RANK, hv)), _const_spec((KV_LORA_RANK, hv)),
                  row(QK_ROPE_DIM // 2), row(QK_ROPE_DIM // 2)],
        out_specs=[row(hv), row(hv), row(hv), row(LANES), row(hv)],
        out_shape=[jax.ShapeDtypeStruct((n_tok, hv), BF16),
                   jax.ShapeDtypeStruct((n_tok, hv), BF16),
                   jax.ShapeDtypeStruct((n_tok, hv), BF16),
                   jax.ShapeDtypeStruct((n_tok, LANES), BF16),
                   jax.ShapeDtypeStruct((n_tok, hv), BF16)],
        compiler_params=_params(1),
        name="mla_proj",
    )(x, g, wd, qn_g, kvn_g, wqn, wqr, wkn, wv, cos, sin)


def _attn_kernel(qn_ref, qr_ref, kn_ref, kr_ref, v_ref, o_ref, m_ref, acc_ref, s_ref):
    qi = pl.program_id(1)
    t = ATTN_TILE
    ones = jnp.ones((2 * t, LANES), BF16)

    def key_rows(start, n):
        return pl.ds(pl.multiple_of(start, t), n)

    def scores(hd, start, n):
        sl = slice(hd * LANES, (hd + 1) * LANES)
        rows = key_rows(start, n)
        q = jnp.concatenate([qn_ref[:, sl], qr_ref[:, sl]], axis=1)
        k = jnp.concatenate([kn_ref[rows, sl], kr_ref[rows, :]], axis=1)
        return lax.dot_general(q, k, (((1,), (1,)), ((), ())),
                               preferred_element_type=F32)

    def update(hd, start, s, diag_col=None, first=False):
        sl = slice(hd * LANES, (hd + 1) * LANES)
        n = s.shape[1]
        if diag_col is not None:
            row = lax.broadcasted_iota(jnp.int32, s.shape, 0)
            col = lax.broadcasted_iota(jnp.int32, s.shape, 1)
            s = jnp.where(col - diag_col <= row, s, -jnp.inf)
        m_new = jnp.broadcast_to(jnp.max(s, axis=1, keepdims=True), (t, LANES))
        if not first:
            m_prev = m_ref[hd]
            m_new = jnp.maximum(m_prev, m_new)
            alpha = jnp.exp2(m_prev - m_new)
        p = jnp.exp2(s - jnp.concatenate([m_new] * (n // LANES), axis=1))
        v_ext = jnp.concatenate([v_ref[key_rows(start, n), sl], ones[:n]], axis=1)
        pv = _dot(p.astype(BF16), v_ext)
        if not first:
            pv = jnp.concatenate([alpha, alpha], axis=1) * acc_ref[hd] + pv
        acc_ref[hd] = pv
        m_ref[hd] = m_new

    def sweep(start, n, diag_col, carry_next, first=False):
        s = s_ref[...]
        if n > t:
            s = jnp.concatenate([s, scores(0, start + t, n - t)], axis=1)
        for hd in range(N_HEADS):
            s_next = None
            if hd + 1 < N_HEADS:
                s_next = scores(hd + 1, start, n)
            elif carry_next:
                s_next = scores(0, start + n, t)
            update(hd, start, s, diag_col, first)
            s = s_next
        if carry_next:
            s_ref[...] = s

    def body(kp, carry):
        sweep(kp * 2 * t, 2 * t, None, True)
        return carry

    s_ref[...] = scores(0, 0, t)
    n_pairs = qi // 2

    @pl.when(n_pairs == 0)
    def _():
        m_ref[...] = jnp.full(m_ref.shape, -jnp.inf, F32)
        acc_ref[...] = jnp.zeros(acc_ref.shape, F32)

    @pl.when(n_pairs > 0)
    def _():
        sweep(0, 2 * t, None, True, first=True)

    lax.fori_loop(1, n_pairs, body, 0)

    @pl.when(qi % 2 == 1)
    def _():
        sweep(n_pairs * 2 * t, 2 * t, t, False)

    @pl.when(qi % 2 == 0)
    def _():
        sweep(qi * t, t, 0, False)

    for hd in range(N_HEADS):
        acc = acc_ref[hd]
        o_ref[:, hd * LANES:(hd + 1) * LANES] = (acc[:, :LANES] / acc[:, LANES:]).astype(BF16)


def _attention(qn, qr, kn, kr, v, batch, seq):
    t = ATTN_TILE
    nq = seq // t
    hv = N_HEADS * LANES
    q_spec = pl.BlockSpec((t, hv), lambda b, i: (b * nq + i, 0))
    seq_spec = lambda w: pl.BlockSpec((seq, w), lambda b, i: (b, 0))
    return pl.pallas_call(
        _attn_kernel,
        grid=(batch, nq),
        in_specs=[q_spec, q_spec, seq_spec(hv), seq_spec(LANES), seq_spec(hv)],
        out_specs=q_spec,
        out_shape=jax.ShapeDtypeStruct((batch * seq, hv), BF16),
        scratch_shapes=[pltpu.VMEM((N_HEADS, t, LANES), F32),
                        pltpu.VMEM((N_HEADS, t, 2 * LANES), F32),
                        pltpu.VMEM((t, t), F32)],
        compiler_params=_params(2),
        name="mla_attention",
    )(qn, qr, kn, kr, v)


def _ffn_stage(x1_ref, h_ref, wg_ref, wu_ref, wdn_ref, gfin_ref, a_ref, o_ref, final_norm,
               side_work):
    h = h_ref[...]
    o_ref[...] = x1_ref[...]
    side_work = list(side_work)
    n_chunks = D_FF // FF_CHUNK
    group = a_ref.shape[1] // FF_CHUNK
    for c in range(n_chunks):
        sl = slice(c * FF_CHUNK, (c + 1) * FF_CHUNK)
        slot = c % group
        gate = _dot(h, wg_ref[:, sl])
        up = _dot(h, wu_ref[:, sl])
        a_ref[:, slot * FF_CHUNK:(slot + 1) * FF_CHUNK] = (
            gate / (1.0 + jnp.exp(-gate)) * up).astype(BF16)
        if side_work:
            side_work.pop(0)()
        if slot == group - 1 or c == n_chunks - 1:
            k0, k1 = (c - slot) * FF_CHUNK, (c + 1) * FF_CHUNK
            down = _dot(a_ref[:, :k1 - k0], wdn_ref[k0:k1, :])
            if c < n_chunks - 1:
                o_ref[...] += down
    for piece in side_work:
        piece()
    out = o_ref[...] + down
    if final_norm:
        out = _rms(out, gfin_ref[...])
    o_ref[...] = out


def _pipelined_step(mixer_pieces, gf_ref, ffn_refs, o_ref, x1_ref, h_slots, final_norm):
    i = pl.program_id(0)

    def mixer_and_norm(cur):
        def ffn_norm():
            h_slots[cur][...] = _rms(x1_ref[...], gf_ref[...]).astype(BF16)

        return mixer_pieces(x1_ref) + [ffn_norm]

    @pl.when(i == 0)
    def _():
        for piece in mixer_and_norm(0):
            piece()

    for cur in (0, 1):
        @pl.when((i > 0) & (i % 2 == cur))
        def _():
            _ffn_stage(x1_ref, h_slots[1 - cur], *ffn_refs, o_ref, final_norm,
                       mixer_and_norm(cur))


def _pool_ffn_kernel(x_ref, halo_ref, g_ref, pw_ref, ps_ref, gf_ref, wg_ref, wu_ref, wdn_ref,
                     gfin_ref, o_ref, ext_ref, st_a, st_b, a_ref, x1_ref, h_a, h_b,
                     *, n_tiles, tiles_per_seq, final_norm):
    tm = x_ref.shape[0]
    blk = jnp.minimum(pl.program_id(0), n_tiles - 1) % tiles_per_seq
    top = POOL_PAD + POOL_HALO
    n_ext = POOL_HALO + tm

    def mixer_pieces(x1_ref):
        def norm():
            for ref in (ext_ref, st_a, st_b):
                ref[:POOL_PAD, :] = jnp.zeros((POOL_PAD, ref.shape[1]), F32)
            ext_ref[top:, :] = _rms(x_ref[...], g_ref[...])
            halo = _rms(halo_ref[...], g_ref[...])
            ext_ref[POOL_PAD:top, :] = jnp.where(blk == 0, 0.0, halo)

        def group(g, win):
            sl = slice(g * POOL_GROUP, (g + 1) * POOL_GROUP)
            acc = ext_ref[POOL_PAD:, sl] + ext_ref[POOL_PAD - 1:POOL_PAD - 1 + n_ext, sl]
            width, stage = 2, 0
            while width < win:
                buf = (st_a, st_b)[stage % 2]
                buf[POOL_PAD:, :] = acc
                acc = buf[POOL_PAD:, :] + buf[POOL_PAD - width:POOL_PAD - width + n_ext, :]
                width, stage = 2 * width, stage + 1
            pos = blk * tm + lax.broadcasted_iota(jnp.int32, (tm, 1), 0)
            avail = (pos + 1).astype(F32)
            p = acc[POOL_HALO:] / jnp.minimum(avail, float(win)) - ext_ref[top:, sl]
            y = _dot(p.astype(BF16), pw_ref[g])
            x1_ref[:, sl] = x_ref[:, sl] + y * ps_ref[:, sl]

        return [norm] + [functools.partial(group, g, win)
                         for g, win in enumerate(POOL_WINDOWS)]

    _pipelined_step(mixer_pieces, gf_ref, (wg_ref, wu_ref, wdn_ref, gfin_ref, a_ref), o_ref,
                    x1_ref, (h_a, h_b), final_norm)


def _wo_ffn_kernel(x_ref, o_attn_ref, wo_ref, gf_ref, wg_ref, wu_ref, wdn_ref,
                   gfin_ref, o_ref, a_ref, x1_ref, h_a, h_b, *, final_norm):
    def mixer_pieces(x1_ref):
        def out_proj():
            x1_ref[...] = x_ref[...] + _dot(o_attn_ref[...], wo_ref[...])

        return [out_proj]

    _pipelined_step(mixer_pieces, gf_ref, (wg_ref, wu_ref, wdn_ref, gfin_ref, a_ref), o_ref,
                    x1_ref, (h_a, h_b), final_norm)


def _layer_spec(layer, shape):
    return pl.BlockSpec((None,) + shape, lambda *_: (layer,) + (0,) * len(shape),
                        pipeline_mode=pl.Buffered(1))


def _ffn_specs(layer):
    return [_const_spec((1, D_MODEL)), _layer_spec(layer, (D_MODEL, D_FF)),
            _layer_spec(layer, (D_MODEL, D_FF)), _layer_spec(layer, (D_FF, D_MODEL)),
            _const_spec((1, D_MODEL))]


def _pipeline_scratch(tm):
    return [pltpu.VMEM((tm, DOWN_PASS_COLS), BF16),
            pltpu.VMEM((tm, D_MODEL), F32),
            pltpu.VMEM((tm, D_MODEL), BF16), pltpu.VMEM((tm, D_MODEL), BF16)]


def _pool_ffn(x, seq, g, pw, ps, gf, wg, wu, wdn, gfin, layer, final_norm):
    n_tok = x.shape[0]
    tm = TOKEN_TILE
    n_tiles = n_tok // tm
    halo_per_tile = tm // POOL_HALO
    tile_in = lambda i: jnp.minimum(i, n_tiles - 1)
    row_in = pl.BlockSpec((tm, D_MODEL), lambda i: (tile_in(i), 0))
    halo = pl.BlockSpec((POOL_HALO, D_MODEL),
                        lambda i: (jnp.maximum(tile_in(i) * halo_per_tile - 1, 0), 0))
    row_out = pl.BlockSpec((tm, D_MODEL), lambda i: (jnp.maximum(i - 1, 0), 0))
    return pl.pallas_call(
        functools.partial(_pool_ffn_kernel, n_tiles=n_tiles, tiles_per_seq=seq // tm,
                          final_norm=final_norm),
        grid=(n_tiles + 1,),
        in_specs=[row_in, halo, _const_spec((1, D_MODEL)),
                  _const_spec((len(POOL_WINDOWS), POOL_GROUP, POOL_GROUP)),
                  _const_spec((1, D_MODEL))] + _ffn_specs(layer),
        out_specs=row_out,
        out_shape=jax.ShapeDtypeStruct((n_tok, D_MODEL), F32),
        scratch_shapes=[pltpu.VMEM((POOL_PAD + POOL_HALO + tm, D_MODEL), F32),
                        pltpu.VMEM((POOL_PAD + POOL_HALO + tm, POOL_GROUP), F32),
                        pltpu.VMEM((POOL_PAD + POOL_HALO + tm, POOL_GROUP), F32)]
        + _pipeline_scratch(tm),
        compiler_params=_params(1),
        name="pool_ffn",
    )(x, x, g, pw, ps, gf, wg, wu, wdn, gfin)


def _wo_ffn(x, o_attn, wo, gf, wg, wu, wdn, gfin, layer, final_norm):
    n_tok = x.shape[0]
    tm = TOKEN_TILE
    n_tiles = n_tok // tm
    row_in = pl.BlockSpec((tm, D_MODEL), lambda i: (jnp.minimum(i, n_tiles - 1), 0))
    row_out = pl.BlockSpec((tm, D_MODEL), lambda i: (jnp.maximum(i - 1, 0), 0))
    return pl.pallas_call(
        functools.partial(_wo_ffn_kernel, final_norm=final_norm),
        grid=(n_tiles + 1,),
        in_specs=[row_in, row_in, _const_spec((D_MODEL, D_MODEL))] + _ffn_specs(layer),
        out_specs=row_out,
        out_shape=jax.ShapeDtypeStruct((n_tok, D_MODEL), F32),
        scratch_shapes=_pipeline_scratch(tm),
        compiler_params=_params(1),
        name="wo_ffn",
    )(x, o_attn, wo, gf, wg, wu, wdn, gfin)


def kernel(x, positions, norm_mix, norm_ffn, norm_final, pool_w, pool_scale, mla_w_down,
           mla_q_norm, mla_w_uq, mla_kv_norm, mla_w_ukv, mla_w_o, ffn_w_gate, ffn_w_up,
           ffn_w_down):
    batch, seq, d = x.shape
    n_tok = batch * seq
    xt = x.reshape(n_tok, d)
    cos, sin = _rope_tables(positions.reshape(n_tok))
    gfin = norm_final.reshape(1, d)
    n_mixers = 2
    wg = ffn_w_gate.astype(BF16)
    wu = ffn_w_up.astype(BF16)
    wdn = ffn_w_down.astype(BF16)
    for i in range(DEPTH):
        j = i // n_mixers
        final = i == DEPTH - 1
        gm = norm_mix[i].reshape(1, d)
        gf = norm_ffn[i].reshape(1, d)
        if i % n_mixers == 0:
            xt = _pool_ffn(xt, seq, gm, pool_w[j].astype(BF16), pool_scale[j].reshape(1, d),
                           gf, wg, wu, wdn, gfin, i, final)
        else:
            rope_pad = LANES - QK_ROPE_DIM
            wd = jnp.pad(mla_w_down[j], ((0, 0), (0, rope_pad))).astype(BF16)
            wq = mla_w_uq[j].reshape(Q_LORA_RANK, N_HEADS, QK_HEAD_DIM)
            wqn = wq[:, :, :QK_NOPE_DIM].reshape(Q_LORA_RANK, -1).astype(BF16)
            wqr = jnp.pad(wq[:, :, QK_NOPE_DIM:], ((0, 0), (0, 0), (0, rope_pad)))
            wqr = wqr.reshape(Q_LORA_RANK, -1).astype(BF16)
            wkv = mla_w_ukv[j].reshape(KV_LORA_RANK, N_HEADS, QK_NOPE_DIM + V_HEAD_DIM)
            wkn = wkv[:, :, :QK_NOPE_DIM].reshape(KV_LORA_RANK, -1).astype(BF16)
            wv = wkv[:, :, QK_NOPE_DIM:].reshape(KV_LORA_RANK, -1).astype(BF16)
            qn, qr, kn, kr, v = _mla_proj(
                xt, gm, wd, mla_q_norm[j].reshape(1, -1), mla_kv_norm[j].reshape(1, -1),
                wqn, wqr, wkn, wv, cos, sin)
            o_attn = _attention(qn, qr, kn, kr, v, batch, seq)
            xt = _wo_ffn(xt, o_attn, mla_w_o[j].astype(BF16), gf, wg, wu, wdn, gfin, i, final)
    return xt.reshape(batch, seq, d)
```

```python
import functools

import numpy as np
import jax
import jax.numpy as jnp
from jax import lax
from jax.experimental import pallas as pl
from jax.experimental.pallas import tpu as pltpu

D_MODEL = 1024
DEPTH = 4
POOL_WINDOWS = (2, 4, 8, 16)
POOL_GROUP = D_MODEL // len(POOL_WINDOWS)
POOL_HALO = 16
POOL_PAD = 8
N_HEADS = 8
QK_NOPE_DIM = 128
QK_ROPE_DIM = 64
V_HEAD_DIM = 128
Q_LORA_RANK = 384
KV_LORA_RANK = 128
QK_HEAD_DIM = QK_NOPE_DIM + QK_ROPE_DIM
ROPE_THETA = 10000.0
D_FF = 2816
RMS_EPS = 1e-6

LANES = 128
MXU_DIM = 256
FF_CHUNK = MXU_DIM
DOWN_PAD = Q_LORA_RANK + KV_LORA_RANK + LANES

TOKEN_TILE = 512
ATTN_TILE = 512
PROJ_TILE = 1024
PROJ_SUB_TILE = 512
ROPE_TILE = 1024
GATE_LOAD_ROWS = 128
DOWN_LOAD_ROWS = 352
VMEM_LIMIT = 56 * 1024 * 1024
ATTN_VMEM_LIMIT = 60 * 1024 * 1024

LOG2_E = 1.4426950408889634
BF16 = jnp.bfloat16
F32 = jnp.float32


def _rms(x, g):
    return x * lax.rsqrt(jnp.mean(x * x, axis=-1, keepdims=True) + RMS_EPS) * g


def _dot(a, b):
    return jnp.dot(a, b, preferred_element_type=F32)


def _const_spec(shape):
    return pl.BlockSpec(shape, lambda *_: (0,) * len(shape), pipeline_mode=pl.Buffered(1))


def _params(n_axes, vmem_limit=VMEM_LIMIT):
    return pltpu.CompilerParams(
        dimension_semantics=("arbitrary",) * n_axes,
        vmem_limit_bytes=vmem_limit)


def _rope_table_kernel(pos_ref, inv_ref, cos_ref, sin_ref):
    ang = pos_ref[...].astype(F32) * inv_ref[...]
    cos_ref[...] = jnp.cos(ang)
    sin_ref[...] = jnp.sin(ang)


def _rope_tables(positions):
    n_tok = positions.shape[0]
    half = QK_ROPE_DIM // 2
    tok_per_row = LANES // half
    n_rows = n_tok // tok_per_row
    inv = 1.0 / (ROPE_THETA ** (np.arange(0, QK_ROPE_DIM, 2, dtype=np.float32) / QK_ROPE_DIM))
    inv_row = np.tile(inv, tok_per_row)[None, :]
    pos_rows = jnp.repeat(positions.reshape(n_rows, tok_per_row), half, axis=1)
    spec = pl.BlockSpec((ROPE_TILE, LANES), lambda i: (i, 0))
    cos, sin = pl.pallas_call(
        _rope_table_kernel,
        grid=(n_rows // ROPE_TILE,),
        in_specs=[spec, _const_spec((1, LANES))],
        out_specs=[spec] * 2,
        out_shape=[jax.ShapeDtypeStruct((n_rows, LANES), F32)] * 2,
        compiler_params=_params(1),
        name="rope_tables",
    )(pos_rows, jnp.asarray(inv_row))
    return cos.reshape(n_tok, half), sin.reshape(n_tok, half)


def _rope(x, cos, sin):
    lane = lax.broadcasted_iota(jnp.int32, x.shape, 1)
    half = QK_ROPE_DIM // 2
    swapped = jnp.where(lane < half,
                        pltpu.roll(x, LANES - half, axis=1),
                        pltpu.roll(x, half, axis=1))
    return x * cos + swapped * sin


def _mla_proj_kernel(x_ref, g_ref, wd_ref, qn_g_ref, kvn_g_ref, wqn_ref, wqr_ref,
                     wkn_ref, wv_ref, cos_ref, sin_ref,
                     qn_ref, qr_ref, kn_ref, kr_ref, v_ref):
    tm = x_ref.shape[0]
    subs = [pl.ds(r, PROJ_SUB_TILE) for r in range(0, tm, PROJ_SUB_TILE)]
    sm_scale = QK_HEAD_DIM ** -0.5 * LOG2_E
    pad = jnp.zeros((PROJ_SUB_TILE, LANES - QK_ROPE_DIM), F32)
    h = [_rms(x_ref[r, :], g_ref[...]).astype(BF16) for r in subs]
    d = [_dot(hh, wd_ref[...]) for hh in h]
    cq = [_rms(dd[:, :Q_LORA_RANK], qn_g_ref[...]).astype(BF16) for dd in d]
    ckv = [_rms(dd[:, Q_LORA_RANK:Q_LORA_RANK + KV_LORA_RANK], kvn_g_ref[...]).astype(BF16)
           for dd in d]
    cos = [jnp.concatenate([cos_ref[r, :], cos_ref[r, :], pad], axis=1) for r in subs]
    sin = [jnp.concatenate([-sin_ref[r, :], sin_ref[r, :], pad], axis=1) for r in subs]
    for i, r in enumerate(subs):
        kr_ref[r, :] = _rope(d[i][:, Q_LORA_RANK + KV_LORA_RANK:], cos[i], sin[i]).astype(BF16)
    for i, r in enumerate(subs):
        qn_ref[r, :] = (_dot(cq[i], wqn_ref[...]) * sm_scale).astype(BF16)
    qr = [_dot(c, wqr_ref[...]) for c in cq]
    for i, r in enumerate(subs):
        for hd in range(N_HEADS):
            sl = slice(hd * LANES, (hd + 1) * LANES)
            qr_ref[r, sl] = (_rope(qr[i][:, sl], cos[i], sin[i]) * sm_scale).astype(BF16)
    for i, r in enumerate(subs):
        kn_ref[r, :] = _dot(ckv[i], wkn_ref[...]).astype(BF16)
    for i, r in enumerate(subs):
        v_ref[r, :] = _dot(ckv[i], wv_ref[...]).astype(BF16)


def _mla_proj(x, g, wd, qn_g, kvn_g, wqn, wqr, wkn, wv, cos, sin):
    n_tok = x.shape[0]
    tm = PROJ_TILE
    hv = N_HEADS * LANES
    row = lambda w: pl.BlockSpec((tm, w), lambda i: (i, 0))
    return pl.pallas_call(
        _mla_proj_kernel,
        grid=(n_tok // tm,),
        in_specs=[row(D_MODEL), _const_spec((1, D_MODEL)), _const_spec((D_MODEL, DOWN_PAD)),
                  _const_spec((1, Q_LORA_RANK)), _const_spec((1, KV_LORA_RANK)),
                  _const_spec((Q_LORA_RANK, hv)), _const_spec((Q_LORA_RANK, hv)),
                  _const_spec((KV_LORA_RANK, hv)), _const_spec((KV_LORA_RANK, hv)),
                  row(QK_ROPE_DIM // 2), row(QK_ROPE_DIM // 2)],
        out_specs=[row(hv), row(hv), row(hv), row(LANES), row(hv)],
        out_shape=[jax.ShapeDtypeStruct((n_tok, hv), BF16),
                   jax.ShapeDtypeStruct((n_tok, hv), BF16),
                   jax.ShapeDtypeStruct((n_tok, hv), BF16),
                   jax.ShapeDtypeStruct((n_tok, LANES), BF16),
                   jax.ShapeDtypeStruct((n_tok, hv), BF16)],
        compiler_params=_params(1),
        name="mla_proj",
    )(x, g, wd, qn_g, kvn_g, wqn, wqr, wkn, wv, cos, sin)


def _attn_kernel(qn_ref, qr_ref, kn_ref, kr_ref, v_ref, o_ref, m_ref, acc_ref, s_ref):
    qi = pl.program_id(1)
    t = ATTN_TILE
    ones = jnp.ones((2 * t, LANES), BF16)

    def key_rows(start, n):
        return pl.ds(pl.multiple_of(start, t), n)

    def scores(hd, start, n):
        sl = slice(hd * LANES, (hd + 1) * LANES)
        rows = key_rows(start, n)
        q = jnp.concatenate([qn_ref[:, sl], qr_ref[:, sl]], axis=1)
        k = jnp.concatenate([kn_ref[rows, sl], kr_ref[rows, :]], axis=1)
        return lax.dot_general(q, k, (((1,), (1,)), ((), ())),
                               preferred_element_type=F32)

    def update(hd, start, s, diag_col=None, first=False):
        sl = slice(hd * LANES, (hd + 1) * LANES)
        n = s.shape[1]
        if diag_col is not None:
            row = lax.broadcasted_iota(jnp.int32, s.shape, 0)
            col = lax.broadcasted_iota(jnp.int32, s.shape, 1)
            s = jnp.where(col - diag_col <= row, s, -jnp.inf)
        m_new = jnp.broadcast_to(jnp.max(s, axis=1, keepdims=True), (t, LANES))
        if not first:
            m_prev = m_ref[hd]
            m_new = jnp.maximum(m_prev, m_new)
            alpha = jnp.exp2(m_prev - m_new)
        p = jnp.exp2(s - jnp.concatenate([m_new] * (n // LANES), axis=1))
        v_ext = jnp.concatenate([v_ref[key_rows(start, n), sl], ones[:n]], axis=1)
        pv = _dot(p.astype(BF16), v_ext)
        if not first:
            pv = jnp.concatenate([alpha, alpha], axis=1) * acc_ref[hd] + pv
        acc_ref[hd] = pv
        m_ref[hd] = m_new

    def sweep(start, n, diag_col, carry_next, first=False):
        s = s_ref[...]
        if n > t:
            s = jnp.concatenate([s, scores(0, start + t, n - t)], axis=1)
        for hd in range(N_HEADS):
            s_next = None
            if hd + 1 < N_HEADS:
                s_next = scores(hd + 1, start, n)
            elif carry_next:
                s_next = scores(0, start + n, t)
            update(hd, start, s, diag_col, first)
            s = s_next
        if carry_next:
            s_ref[...] = s

    def body(kp, carry):
        sweep(kp * 2 * t, 2 * t, None, True)
        return carry

    s_ref[...] = scores(0, 0, t)
    n_pairs = qi // 2

    @pl.when(n_pairs == 0)
    def _():
        m_ref[...] = jnp.full(m_ref.shape, -jnp.inf, F32)
        acc_ref[...] = jnp.zeros(acc_ref.shape, F32)

    @pl.when(n_pairs > 0)
    def _():
        sweep(0, 2 * t, None, True, first=True)

    lax.fori_loop(1, n_pairs, body, 0)

    @pl.when(qi % 2 == 1)
    def _():
        sweep(n_pairs * 2 * t, 2 * t, t, False)

    @pl.when(qi % 2 == 0)
    def _():
        sweep(qi * t, t, 0, False)

    for hd in range(N_HEADS):
        acc = acc_ref[hd]
        o_ref[:, hd * LANES:(hd + 1) * LANES] = (acc[:, :LANES] / acc[:, LANES:]).astype(BF16)


def _attention(qn, qr, kn, kr, v, batch, seq):
    t = ATTN_TILE
    nq = seq // t
    hv = N_HEADS * LANES
    q_spec = pl.BlockSpec((t, hv), lambda b, i: (b * nq + i, 0))
    seq_spec = lambda w: pl.BlockSpec((seq, w), lambda b, i: (b, 0))
    return pl.pallas_call(
        _attn_kernel,
        grid=(batch, nq),
        in_specs=[q_spec, q_spec, seq_spec(hv), seq_spec(LANES), seq_spec(hv)],
        out_specs=q_spec,
        out_shape=jax.ShapeDtypeStruct((batch * seq, hv), BF16),
        scratch_shapes=[pltpu.VMEM((N_HEADS, t, LANES), F32),
                        pltpu.VMEM((N_HEADS, t, 2 * LANES), F32),
                        pltpu.VMEM((t, t), F32)],
        compiler_params=_params(2, ATTN_VMEM_LIMIT),
        name="mla_attention",
    )(qn, qr, kn, kr, v)


def _ffn_stage(x1_ref, h_ref, wg_ref, wu_ref, wdn_ref, gfin_ref, a_ref, o_ref, final_norm,
               side_work):
    h = h_ref[...]
    side_work = list(side_work)
    for c in range(D_FF // FF_CHUNK):
        sl = slice(c * FF_CHUNK, (c + 1) * FF_CHUNK)
        gate = _dot(h, wg_ref[:, sl])
        up = _dot(h, wu_ref[:, sl])
        a_ref[:, sl] = (gate / (1.0 + jnp.exp(-gate)) * up).astype(BF16)
        if side_work:
            side_work.pop(0)()
    for piece in side_work:
        piece()
    out = x1_ref[...] + _dot(a_ref[...], wdn_ref[...])
    if final_norm:
        out = _rms(out, gfin_ref[...])
    o_ref[...] = out


def _load_ffn_weights(layer, hbm_refs, vmem_refs, stage_gate, stage_down, sem):
    jobs = []
    for src, dst, stage, rows in zip(hbm_refs, vmem_refs,
                                     (stage_gate, stage_gate, stage_down),
                                     (GATE_LOAD_ROWS, GATE_LOAD_ROWS, DOWN_LOAD_ROWS)):
        for r in range(0, dst.shape[0], rows):
            jobs.append((src.at[layer, pl.ds(r, rows), :], dst.at[pl.ds(r, rows), :], stage))

    def copy(j):
        src, _, stage = jobs[j]
        return pltpu.make_async_copy(src, stage.at[j % 2], sem.at[j % 2])

    copy(0).start()
    for j, (_, dst, stage) in enumerate(jobs):
        if j + 1 < len(jobs):
            copy(j + 1).start()
        copy(j).wait()
        dst[...] = stage[j % 2].astype(BF16)


def _pipelined_step(mixer_pieces, gf_ref, ffn_refs, o_ref, x1_slots, h_slots, final_norm,
                    load_weights):
    i = pl.program_id(0)

    def mixer_and_norm(cur):
        def ffn_norm():
            h_slots[cur][...] = _rms(x1_slots[cur][...], gf_ref[...]).astype(BF16)

        return mixer_pieces(x1_slots[cur]) + [ffn_norm]

    @pl.when(i == 0)
    def _():
        load_weights()
        for piece in mixer_and_norm(0):
            piece()

    for cur in (0, 1):
        @pl.when((i > 0) & (i % 2 == cur))
        def _():
            _ffn_stage(x1_slots[1 - cur], h_slots[1 - cur], *ffn_refs, o_ref, final_norm,
                       mixer_and_norm(cur))


def _pool_ffn_kernel(x_ref, halo_ref, g_ref, pw_ref, ps_ref, gf_ref, wg_hbm, wu_hbm, wdn_hbm,
                     gfin_ref, o_ref, ext_ref, st_a, st_b, a_ref, x1_a, x1_b, h_a, h_b,
                     wg_ref, wu_ref, wdn_ref, stage_gate, stage_down, sem,
                     *, n_tiles, tiles_per_seq, layer, final_norm):
    tm = x_ref.shape[0]
    blk = jnp.minimum(pl.program_id(0), n_tiles - 1) % tiles_per_seq
    top = POOL_PAD + POOL_HALO
    n_ext = POOL_HALO + tm

    def mixer_pieces(x1_ref):
        def norm():
            for ref in (ext_ref, st_a, st_b):
                ref[:POOL_PAD, :] = jnp.zeros((POOL_PAD, ref.shape[1]), F32)
            ext_ref[top:, :] = _rms(x_ref[...], g_ref[...])
            halo = _rms(halo_ref[...], g_ref[...])
            ext_ref[POOL_PAD:top, :] = jnp.where(blk == 0, 0.0, halo)

        def group(g, win):
            sl = slice(g * POOL_GROUP, (g + 1) * POOL_GROUP)
            acc = ext_ref[POOL_PAD:, sl] + ext_ref[POOL_PAD - 1:POOL_PAD - 1 + n_ext, sl]
            width, stage = 2, 0
            while width < win:
                buf = (st_a, st_b)[stage % 2]
                buf[POOL_PAD:, :] = acc
                acc = buf[POOL_PAD:, :] + buf[POOL_PAD - width:POOL_PAD - width + n_ext, :]
                width, stage = 2 * width, stage + 1
            pos = blk * tm + lax.broadcasted_iota(jnp.int32, (tm, 1), 0)
            avail = (pos + 1).astype(F32)
            p = acc[POOL_HALO:] / jnp.minimum(avail, float(win)) - ext_ref[top:, sl]
            y = _dot(p.astype(BF16), pw_ref[g])
            x1_ref[:, sl] = x_ref[:, sl] + y * ps_ref[:, sl]

        return [norm] + [functools.partial(group, g, win)
                         for g, win in enumerate(POOL_WINDOWS)]

    load_weights = functools.partial(
        _load_ffn_weights, layer, (wg_hbm, wu_hbm, wdn_hbm), (wg_ref, wu_ref, wdn_ref),
        stage_gate, stage_down, sem)
    _pipelined_step(mixer_pieces, gf_ref, (wg_ref, wu_ref, wdn_ref, gfin_ref, a_ref), o_ref,
                    (x1_a, x1_b), (h_a, h_b), final_norm, load_weights)


def _wo_ffn_kernel(x_ref, o_attn_ref, wo_ref, gf_ref, wg_hbm, wu_hbm, wdn_hbm,
                   gfin_ref, o_ref, a_ref, x1_a, x1_b, h_a, h_b,
                   wg_ref, wu_ref, wdn_ref, stage_gate, stage_down, sem, *, layer, final_norm):
    def mixer_pieces(x1_ref):
        def out_proj():
            x1_ref[...] = x_ref[...] + _dot(o_attn_ref[...], wo_ref[...])

        return [out_proj]

    load_weights = functools.partial(
        _load_ffn_weights, layer, (wg_hbm, wu_hbm, wdn_hbm), (wg_ref, wu_ref, wdn_ref),
        stage_gate, stage_down, sem)
    _pipelined_step(mixer_pieces, gf_ref, (wg_ref, wu_ref, wdn_ref, gfin_ref, a_ref), o_ref,
                    (x1_a, x1_b), (h_a, h_b), final_norm, load_weights)


def _ffn_specs():
    hbm = pl.BlockSpec(memory_space=pl.ANY)
    return [_const_spec((1, D_MODEL)), hbm, hbm, hbm, _const_spec((1, D_MODEL))]


def _pipeline_scratch(tm):
    return [pltpu.VMEM((tm, D_FF), BF16),
            pltpu.VMEM((tm, D_MODEL), F32), pltpu.VMEM((tm, D_MODEL), F32),
            pltpu.VMEM((tm, D_MODEL), BF16), pltpu.VMEM((tm, D_MODEL), BF16),
            pltpu.VMEM((D_MODEL, D_FF), BF16), pltpu.VMEM((D_MODEL, D_FF), BF16),
            pltpu.VMEM((D_FF, D_MODEL), BF16),
            pltpu.VMEM((2, GATE_LOAD_ROWS, D_FF), F32),
            pltpu.VMEM((2, DOWN_LOAD_ROWS, D_MODEL), F32),
            pltpu.SemaphoreType.DMA((2,))]


def _pool_ffn(x, seq, g, pw, ps, gf, wg, wu, wdn, gfin, layer, final_norm):
    n_tok = x.shape[0]
    tm = TOKEN_TILE
    n_tiles = n_tok // tm
    halo_per_tile = tm // POOL_HALO
    tile_in = lambda i: jnp.minimum(i, n_tiles - 1)
    row_in = pl.BlockSpec((tm, D_MODEL), lambda i: (tile_in(i), 0))
    halo = pl.BlockSpec((POOL_HALO, D_MODEL),
                        lambda i: (jnp.maximum(tile_in(i) * halo_per_tile - 1, 0), 0))
    row_out = pl.BlockSpec((tm, D_MODEL), lambda i: (jnp.maximum(i - 1, 0), 0))
    return pl.pallas_call(
        functools.partial(_pool_ffn_kernel, n_tiles=n_tiles, tiles_per_seq=seq // tm,
                          layer=layer, final_norm=final_norm),
        grid=(n_tiles + 1,),
        in_specs=[row_in, halo, _const_spec((1, D_MODEL)),
                  _const_spec((len(POOL_WINDOWS), POOL_GROUP, POOL_GROUP)),
                  _const_spec((1, D_MODEL))] + _ffn_specs(),
        out_specs=row_out,
        out_shape=jax.ShapeDtypeStruct((n_tok, D_MODEL), F32),
        scratch_shapes=[pltpu.VMEM((POOL_PAD + POOL_HALO + tm, D_MODEL), F32),
                        pltpu.VMEM((POOL_PAD + POOL_HALO + tm, POOL_GROUP), F32),
                        pltpu.VMEM((POOL_PAD + POOL_HALO + tm, POOL_GROUP), F32)]
        + _pipeline_scratch(tm),
        compiler_params=_params(1),
        name="pool_ffn",
    )(x, x, g, pw, ps, gf, wg, wu, wdn, gfin)


def _wo_ffn(x, o_attn, wo, gf, wg, wu, wdn, gfin, layer, final_norm):
    n_tok = x.shape[0]
    tm = TOKEN_TILE
    n_tiles = n_tok // tm
    row_in = pl.BlockSpec((tm, D_MODEL), lambda i: (jnp.minimum(i, n_tiles - 1), 0))
    row_out = pl.BlockSpec((tm, D_MODEL), lambda i: (jnp.maximum(i - 1, 0), 0))
    return pl.pallas_call(
        functools.partial(_wo_ffn_kernel, layer=layer, final_norm=final_norm),
        grid=(n_tiles + 1,),
        in_specs=[row_in, row_in, _const_spec((D_MODEL, D_MODEL))] + _ffn_specs(),
        out_specs=row_out,
        out_shape=jax.ShapeDtypeStruct((n_tok, D_MODEL), F32),
        scratch_shapes=_pipeline_scratch(tm),
        compiler_params=_params(1),
        name="wo_ffn",
    )(x, o_attn, wo, gf, wg, wu, wdn, gfin)


def kernel(x, positions, norm_mix, norm_ffn, norm_final, pool_w, pool_scale, mla_w_down,
           mla_q_norm, mla_w_uq, mla_kv_norm, mla_w_ukv, mla_w_o, ffn_w_gate, ffn_w_up,
           ffn_w_down):
    batch, seq, d = x.shape
    n_tok = batch * seq
    xt = x.reshape(n_tok, d)
    cos, sin = _rope_tables(positions.reshape(n_tok))
    gfin = norm_final.reshape(1, d)
    n_mixers = 2
    wg, wu, wdn = ffn_w_gate, ffn_w_up, ffn_w_down
    for i in range(DEPTH):
        j = i // n_mixers
        final = i == DEPTH - 1
        gm = norm_mix[i].reshape(1, d)
        gf = norm_ffn[i].reshape(1, d)
        if i % n_mixers == 0:
            xt = _pool_ffn(xt, seq, gm, pool_w[j].astype(BF16), pool_scale[j].reshape(1, d),
                           gf, wg, wu, wdn, gfin, i, final)
        else:
            rope_pad = LANES - QK_ROPE_DIM
            wd = jnp.pad(mla_w_down[j], ((0, 0), (0, rope_pad))).astype(BF16)
            wq = mla_w_uq[j].reshape(Q_LORA_RANK, N_HEADS, QK_HEAD_DIM)
            wqn = wq[:, :, :QK_NOPE_DIM].reshape(Q_LORA_RANK, -1).astype(BF16)
            wqr = jnp.pad(wq[:, :, QK_NOPE_DIM:], ((0, 0), (0, 0), (0, rope_pad)))
            wqr = wqr.reshape(Q_LORA_RANK, -1).astype(BF16)
            wkv = mla_w_ukv[j].reshape(KV_LORA_RANK, N_HEADS, QK_NOPE_DIM + V_HEAD_DIM)
            wkn = wkv[:, :, :QK_NOPE_DIM].reshape(KV_LORA_RANK, -1).astype(BF16)
            wv = wkv[:, :, QK_NOPE_DIM:].reshape(KV_LORA_RANK, -1).astype(BF16)
            qn, qr, kn, kr, v = _mla_proj(
                xt, gm, wd, mla_q_norm[j].reshape(1, -1), mla_kv_norm[j].reshape(1, -1),
                wqn, wqr, wkn, wv, cos, sin)
            o_attn = _attention(qn, qr, kn, kr, v, batch, seq)
            xt = _wo_ffn(xt, o_attn, mla_w_o[j].astype(BF16), gf, wg, wu, wdn, gfin, i, final)
    return xt.reshape(batch, seq, d)
```

```python
import functools

import numpy as np
import jax
import jax.numpy as jnp
from jax import lax
from jax.experimental import pallas as pl
from jax.experimental.pallas import tpu as pltpu

D_MODEL = 1024
DEPTH = 4
POOL_WINDOWS = (2, 4, 8, 16)
POOL_GROUP = D_MODEL // len(POOL_WINDOWS)
POOL_HALO = 16
POOL_PAD = 8
N_HEADS = 8
QK_NOPE_DIM = 128
QK_ROPE_DIM = 64
V_HEAD_DIM = 128
Q_LORA_RANK = 384
KV_LORA_RANK = 128
QK_HEAD_DIM = QK_NOPE_DIM + QK_ROPE_DIM
ROPE_THETA = 10000.0
D_FF = 2816
RMS_EPS = 1e-6

LANES = 128
MXU_DIM = 256
FF_CHUNK = MXU_DIM
DOWN_PAD = Q_LORA_RANK + KV_LORA_RANK + LANES

TOKEN_TILE = 512
ATTN_TILE = 512
PROJ_TILE = 1024
PROJ_SUB_TILE = 512
ROPE_TILE = 1024
GATE_LOAD_ROWS = 128
DOWN_LOAD_ROWS = 352
LOAD_SLOTS = 4
VMEM_LIMIT = 56 * 1024 * 1024
ATTN_VMEM_LIMIT = 60 * 1024 * 1024

LOG2_E = 1.4426950408889634
BF16 = jnp.bfloat16
F32 = jnp.float32


def _rms(x, g):
    return x * lax.rsqrt(jnp.mean(x * x, axis=-1, keepdims=True) + RMS_EPS) * g


def _dot(a, b):
    return jnp.dot(a, b, preferred_element_type=F32)


def _const_spec(shape):
    return pl.BlockSpec(shape, lambda *_: (0,) * len(shape), pipeline_mode=pl.Buffered(1))


def _params(n_axes, vmem_limit=VMEM_LIMIT):
    return pltpu.CompilerParams(
        dimension_semantics=("arbitrary",) * n_axes,
        vmem_limit_bytes=vmem_limit)


def _rope_table_kernel(pos_ref, inv_ref, cos_ref, sin_ref):
    ang = pos_ref[...].astype(F32) * inv_ref[...]
    cos_ref[...] = jnp.cos(ang)
    sin_ref[...] = jnp.sin(ang)


def _rope_tables(positions):
    n_tok = positions.shape[0]
    half = QK_ROPE_DIM // 2
    tok_per_row = LANES // half
    n_rows = n_tok // tok_per_row
    inv = 1.0 / (ROPE_THETA ** (np.arange(0, QK_ROPE_DIM, 2, dtype=np.float32) / QK_ROPE_DIM))
    inv_row = np.tile(inv, tok_per_row)[None, :]
    pos_rows = jnp.repeat(positions.reshape(n_rows, tok_per_row), half, axis=1)
    spec = pl.BlockSpec((ROPE_TILE, LANES), lambda i: (i, 0))
    cos, sin = pl.pallas_call(
        _rope_table_kernel,
        grid=(n_rows // ROPE_TILE,),
        in_specs=[spec, _const_spec((1, LANES))],
        out_specs=[spec] * 2,
        out_shape=[jax.ShapeDtypeStruct((n_rows, LANES), F32)] * 2,
        compiler_params=_params(1),
        name="rope_tables",
    )(pos_rows, jnp.asarray(inv_row))
    return cos.reshape(n_tok, half), sin.reshape(n_tok, half)


def _rope(x, cos, sin):
    lane = lax.broadcasted_iota(jnp.int32, x.shape, 1)
    half = QK_ROPE_DIM // 2
    swapped = jnp.where(lane < half,
                        pltpu.roll(x, LANES - half, axis=1),
                        pltpu.roll(x, half, axis=1))
    return x * cos + swapped * sin


def _mla_proj_kernel(x_ref, g_ref, wd_ref, qn_g_ref, kvn_g_ref, wqn_ref, wqr_ref,
                     wkn_ref, wv_ref, cos_ref, sin_ref,
                     qn_ref, qr_ref, kn_ref, kr_ref, v_ref):
    tm = x_ref.shape[0]
    subs = [pl.ds(r, PROJ_SUB_TILE) for r in range(0, tm, PROJ_SUB_TILE)]
    sm_scale = QK_HEAD_DIM ** -0.5 * LOG2_E
    pad = jnp.zeros((PROJ_SUB_TILE, LANES - QK_ROPE_DIM), F32)
    h = [_rms(x_ref[r, :], g_ref[...]).astype(BF16) for r in subs]
    d = [_dot(hh, wd_ref[...]) for hh in h]
    cq = [_rms(dd[:, :Q_LORA_RANK], qn_g_ref[...]).astype(BF16) for dd in d]
    ckv = [_rms(dd[:, Q_LORA_RANK:Q_LORA_RANK + KV_LORA_RANK], kvn_g_ref[...]).astype(BF16)
           for dd in d]
    cos = [jnp.concatenate([cos_ref[r, :], cos_ref[r, :], pad], axis=1) for r in subs]
    sin = [jnp.concatenate([-sin_ref[r, :], sin_ref[r, :], pad], axis=1) for r in subs]
    for i, r in enumerate(subs):
        kr_ref[r, :] = _rope(d[i][:, Q_LORA_RANK + KV_LORA_RANK:], cos[i], sin[i]).astype(BF16)
    for i, r in enumerate(subs):
        qn_ref[r, :] = (_dot(cq[i], wqn_ref[...]) * sm_scale).astype(BF16)
    qr = [_dot(c, wqr_ref[...]) for c in cq]
    for i, r in enumerate(subs):
        for hd in range(N_HEADS):
            sl = slice(hd * LANES, (hd + 1) * LANES)
            qr_ref[r, sl] = (_rope(qr[i][:, sl], cos[i], sin[i]) * sm_scale).astype(BF16)
    for i, r in enumerate(subs):
        kn_ref[r, :] = _dot(ckv[i], wkn_ref[...]).astype(BF16)
    for i, r in enumerate(subs):
        v_ref[r, :] = _dot(ckv[i], wv_ref[...]).astype(BF16)


def _mla_proj(x, g, wd, qn_g, kvn_g, wqn, wqr, wkn, wv, cos, sin):
    n_tok = x.shape[0]
    tm = PROJ_TILE
    hv = N_HEADS * LANES
    row = lambda w: pl.BlockSpec((tm, w), lambda i: (i, 0))
    return pl.pallas_call(
        _mla_proj_kernel,
        grid=(n_tok // tm,),
        in_specs=[row(D_MODEL), _const_spec((1, D_MODEL)), _const_spec((D_MODEL, DOWN_PAD)),
                  _const_spec((1, Q_LORA_RANK)), _const_spec((1, KV_LORA_RANK)),
                  _const_spec((Q_LORA_RANK, hv)), _const_spec((Q_LORA_RANK, hv)),
                  _const_spec((KV_LORA_RANK, hv)), _const_spec((KV_LORA_RANK, hv)),
                  row(QK_ROPE_DIM // 2), row(QK_ROPE_DIM // 2)],
        out_specs=[row(hv), row(hv), row(hv), row(LANES), row(hv)],
        out_shape=[jax.ShapeDtypeStruct((n_tok, hv), BF16),
                   jax.ShapeDtypeStruct((n_tok, hv), BF16),
                   jax.ShapeDtypeStruct((n_tok, hv), BF16),
                   jax.ShapeDtypeStruct((n_tok, LANES), BF16),
                   jax.ShapeDtypeStruct((n_tok, hv), BF16)],
        compiler_params=_params(1),
        name="mla_proj",
    )(x, g, wd, qn_g, kvn_g, wqn, wqr, wkn, wv, cos, sin)


def _attn_kernel(qn_ref, qr_ref, kn_ref, kr_ref, v_ref, o_ref, m_ref, acc_ref, s_ref):
    qi = pl.program_id(1)
    t = ATTN_TILE
    ones = jnp.ones((2 * t, LANES), BF16)

    def key_rows(start, n):
        return pl.ds(pl.multiple_of(start, t), n)

    def scores(hd, start, n):
        sl = slice(hd * LANES, (hd + 1) * LANES)
        rows = key_rows(start, n)
        q = jnp.concatenate([qn_ref[:, sl], qr_ref[:, sl]], axis=1)
        k = jnp.concatenate([kn_ref[rows, sl], kr_ref[rows, :]], axis=1)
        return lax.dot_general(q, k, (((1,), (1,)), ((), ())),
                               preferred_element_type=F32)

    def update(hd, start, s, diag_col=None, first=False):
        sl = slice(hd * LANES, (hd + 1) * LANES)
        n = s.shape[1]
        if diag_col is not None:
            row = lax.broadcasted_iota(jnp.int32, s.shape, 0)
            col = lax.broadcasted_iota(jnp.int32, s.shape, 1)
            s = jnp.where(col - diag_col <= row, s, -jnp.inf)
        m_new = jnp.broadcast_to(jnp.max(s, axis=1, keepdims=True), (t, LANES))
        if not first:
            m_prev = m_ref[hd]
            m_new = jnp.maximum(m_prev, m_new)
            alpha = jnp.exp2(m_prev - m_new)
        p = jnp.exp2(s - jnp.concatenate([m_new] * (n // LANES), axis=1))
        v_ext = jnp.concatenate([v_ref[key_rows(start, n), sl], ones[:n]], axis=1)
        pv = _dot(p.astype(BF16), v_ext)
        if not first:
            pv = jnp.concatenate([alpha, alpha], axis=1) * acc_ref[hd] + pv
        acc_ref[hd] = pv
        m_ref[hd] = m_new

    def sweep(start, n, diag_col, carry_next, first=False):
        s = s_ref[...]
        if n > t:
            s = jnp.concatenate([s, scores(0, start + t, n - t)], axis=1)
        for hd in range(N_HEADS):
            s_next = None
            if hd + 1 < N_HEADS:
                s_next = scores(hd + 1, start, n)
            elif carry_next:
                s_next = scores(0, start + n, t)
            update(hd, start, s, diag_col, first)
            s = s_next
        if carry_next:
            s_ref[...] = s

    def body(kp, carry):
        sweep(kp * 2 * t, 2 * t, None, True)
        return carry

    s_ref[...] = scores(0, 0, t)
    n_pairs = qi // 2

    @pl.when(n_pairs == 0)
    def _():
        m_ref[...] = jnp.full(m_ref.shape, -jnp.inf, F32)
        acc_ref[...] = jnp.zeros(acc_ref.shape, F32)

    @pl.when(n_pairs > 0)
    def _():
        sweep(0, 2 * t, None, True, first=True)

    lax.fori_loop(1, n_pairs, body, 0)

    @pl.when(qi % 2 == 1)
    def _():
        sweep(n_pairs * 2 * t, 2 * t, t, False)

    @pl.when(qi % 2 == 0)
    def _():
        sweep(qi * t, t, 0, False)

    for hd in range(N_HEADS):
        acc = acc_ref[hd]
        o_ref[:, hd * LANES:(hd + 1) * LANES] = (acc[:, :LANES] / acc[:, LANES:]).astype(BF16)


def _attention(qn, qr, kn, kr, v, batch, seq):
    t = ATTN_TILE
    nq = seq // t
    hv = N_HEADS * LANES
    q_spec = pl.BlockSpec((t, hv), lambda b, i: (b * nq + i, 0))
    seq_spec = lambda w: pl.BlockSpec((seq, w), lambda b, i: (b, 0))
    return pl.pallas_call(
        _attn_kernel,
        grid=(batch, nq),
        in_specs=[q_spec, q_spec, seq_spec(hv), seq_spec(LANES), seq_spec(hv)],
        out_specs=q_spec,
        out_shape=jax.ShapeDtypeStruct((batch * seq, hv), BF16),
        scratch_shapes=[pltpu.VMEM((N_HEADS, t, LANES), F32),
                        pltpu.VMEM((N_HEADS, t, 2 * LANES), F32),
                        pltpu.VMEM((t, t), F32)],
        compiler_params=_params(2, ATTN_VMEM_LIMIT),
        name="mla_attention",
    )(qn, qr, kn, kr, v)


def _ffn_stage(x1_ref, h_ref, wg_ref, wu_ref, wdn_ref, gfin_ref, a_ref, o_ref, final_norm,
               side_work):
    h = h_ref[...]
    side_work = list(side_work)
    for c in range(D_FF // FF_CHUNK):
        sl = slice(c * FF_CHUNK, (c + 1) * FF_CHUNK)
        gate = _dot(h, wg_ref[:, sl])
        up = _dot(h, wu_ref[:, sl])
        a_ref[:, sl] = (gate / (1.0 + jnp.exp(-gate)) * up).astype(BF16)
        if side_work:
            side_work.pop(0)()
    for piece in side_work:
        piece()
    out = x1_ref[...] + _dot(a_ref[...], wdn_ref[...])
    if final_norm:
        out = _rms(out, gfin_ref[...])
    o_ref[...] = out


def _load_ffn_weights(layer, hbm_refs, vmem_refs, stage_gate, stage_down, sem):
    jobs = []
    for src, dst, stage, rows in zip(hbm_refs, vmem_refs,
                                     (stage_gate, stage_gate, stage_down),
                                     (GATE_LOAD_ROWS, GATE_LOAD_ROWS, DOWN_LOAD_ROWS)):
        for r in range(0, dst.shape[0], rows):
            jobs.append((src.at[layer, pl.ds(r, rows), :], dst.at[pl.ds(r, rows), :], stage))

    def copy(j):
        src, _, stage = jobs[j]
        return pltpu.make_async_copy(src, stage.at[j % LOAD_SLOTS], sem.at[j % LOAD_SLOTS])

    for j in range(LOAD_SLOTS - 1):
        copy(j).start()
    for j, (_, dst, stage) in enumerate(jobs):
        if j + LOAD_SLOTS - 1 < len(jobs):
            copy(j + LOAD_SLOTS - 1).start()
        copy(j).wait()
        dst[...] = stage[j % LOAD_SLOTS].astype(BF16)


def _pipelined_step(mixer_pieces, gf_ref, ffn_refs, o_ref, x1_slots, h_slots, final_norm,
                    load_weights):
    i = pl.program_id(0)

    def mixer_and_norm(cur):
        def ffn_norm():
            h_slots[cur][...] = _rms(x1_slots[cur][...], gf_ref[...]).astype(BF16)

        return mixer_pieces(x1_slots[cur]) + [ffn_norm]

    @pl.when(i == 0)
    def _():
        load_weights()
        for piece in mixer_and_norm(0):
            piece()

    for cur in (0, 1):
        @pl.when((i > 0) & (i % 2 == cur))
        def _():
            _ffn_stage(x1_slots[1 - cur], h_slots[1 - cur], *ffn_refs, o_ref, final_norm,
                       mixer_and_norm(cur))


def _pool_ffn_kernel(x_ref, halo_ref, g_ref, pw_ref, ps_ref, gf_ref, wg_hbm, wu_hbm, wdn_hbm,
                     gfin_ref, o_ref, ext_ref, st_a, st_b, a_ref, x1_a, x1_b, h_a, h_b,
                     wg_ref, wu_ref, wdn_ref, stage_gate, stage_down, sem,
                     *, n_tiles, tiles_per_seq, layer, final_norm):
    tm = x_ref.shape[0]
    blk = jnp.minimum(pl.program_id(0), n_tiles - 1) % tiles_per_seq
    top = POOL_PAD + POOL_HALO
    n_ext = POOL_HALO + tm

    def mixer_pieces(x1_ref):
        def norm():
            for ref in (ext_ref, st_a, st_b):
                ref[:POOL_PAD, :] = jnp.zeros((POOL_PAD, ref.shape[1]), F32)
            ext_ref[top:, :] = _rms(x_ref[...], g_ref[...])
            halo = _rms(halo_ref[...], g_ref[...])
            ext_ref[POOL_PAD:top, :] = jnp.where(blk == 0, 0.0, halo)

        def group(g, win):
            sl = slice(g * POOL_GROUP, (g + 1) * POOL_GROUP)
            acc = ext_ref[POOL_PAD:, sl] + ext_ref[POOL_PAD - 1:POOL_PAD - 1 + n_ext, sl]
            width, stage = 2, 0
            while width < win:
                buf = (st_a, st_b)[stage % 2]
                buf[POOL_PAD:, :] = acc
                acc = buf[POOL_PAD:, :] + buf[POOL_PAD - width:POOL_PAD - width + n_ext, :]
                width, stage = 2 * width, stage + 1
            pos = blk * tm + lax.broadcasted_iota(jnp.int32, (tm, 1), 0)
            avail = (pos + 1).astype(F32)
            p = acc[POOL_HALO:] / jnp.minimum(avail, float(win)) - ext_ref[top:, sl]
            y = _dot(p.astype(BF16), pw_ref[g])
            x1_ref[:, sl] = x_ref[:, sl] + y * ps_ref[:, sl]

        return [norm] + [functools.partial(group, g, win)
                         for g, win in enumerate(POOL_WINDOWS)]

    load_weights = functools.partial(
        _load_ffn_weights, layer, (wg_hbm, wu_hbm, wdn_hbm), (wg_ref, wu_ref, wdn_ref),
        stage_gate, stage_down, sem)
    _pipelined_step(mixer_pieces, gf_ref, (wg_ref, wu_ref, wdn_ref, gfin_ref, a_ref), o_ref,
                    (x1_a, x1_b), (h_a, h_b), final_norm, load_weights)


def _wo_ffn_kernel(x_ref, o_attn_ref, wo_ref, gf_ref, wg_hbm, wu_hbm, wdn_hbm,
                   gfin_ref, o_ref, a_ref, x1_a, x1_b, h_a, h_b,
                   wg_ref, wu_ref, wdn_ref, stage_gate, stage_down, sem, *, layer, final_norm):
    def mixer_pieces(x1_ref):
        def out_proj():
            x1_ref[...] = x_ref[...] + _dot(o_attn_ref[...], wo_ref[...])

        return [out_proj]

    load_weights = functools.partial(
        _load_ffn_weights, layer, (wg_hbm, wu_hbm, wdn_hbm), (wg_ref, wu_ref, wdn_ref),
        stage_gate, stage_down, sem)
    _pipelined_step(mixer_pieces, gf_ref, (wg_ref, wu_ref, wdn_ref, gfin_ref, a_ref), o_ref,
                    (x1_a, x1_b), (h_a, h_b), final_norm, load_weights)


def _ffn_specs():
    hbm = pl.BlockSpec(memory_space=pl.ANY)
    return [_const_spec((1, D_MODEL)), hbm, hbm, hbm, _const_spec((1, D_MODEL))]


def _pipeline_scratch(tm):
    return [pltpu.VMEM((tm, D_FF), BF16),
            pltpu.VMEM((tm, D_MODEL), F32), pltpu.VMEM((tm, D_MODEL), F32),
            pltpu.VMEM((tm, D_MODEL), BF16), pltpu.VMEM((tm, D_MODEL), BF16),
            pltpu.VMEM((D_MODEL, D_FF), BF16), pltpu.VMEM((D_MODEL, D_FF), BF16),
            pltpu.VMEM((D_FF, D_MODEL), BF16),
            pltpu.VMEM((LOAD_SLOTS, GATE_LOAD_ROWS, D_FF), F32),
            pltpu.VMEM((LOAD_SLOTS, DOWN_LOAD_ROWS, D_MODEL), F32),
            pltpu.SemaphoreType.DMA((LOAD_SLOTS,))]


def _pool_ffn(x, seq, g, pw, ps, gf, wg, wu, wdn, gfin, layer, final_norm):
    n_tok = x.shape[0]
    tm = TOKEN_TILE
    n_tiles = n_tok // tm
    halo_per_tile = tm // POOL_HALO
    tile_in = lambda i: jnp.minimum(i, n_tiles - 1)
    row_in = pl.BlockSpec((tm, D_MODEL), lambda i: (tile_in(i), 0))
    halo = pl.BlockSpec((POOL_HALO, D_MODEL),
                        lambda i: (jnp.maximum(tile_in(i) * halo_per_tile - 1, 0), 0))
    row_out = pl.BlockSpec((tm, D_MODEL), lambda i: (jnp.maximum(i - 1, 0), 0))
    return pl.pallas_call(
        functools.partial(_pool_ffn_kernel, n_tiles=n_tiles, tiles_per_seq=seq // tm,
                          layer=layer, final_norm=final_norm),
        grid=(n_tiles + 1,),
        in_specs=[row_in, halo, _const_spec((1, D_MODEL)),
                  _const_spec((len(POOL_WINDOWS), POOL_GROUP, POOL_GROUP)),
                  _const_spec((1, D_MODEL))] + _ffn_specs(),
        out_specs=row_out,
        out_shape=jax.ShapeDtypeStruct((n_tok, D_MODEL), F32),
        scratch_shapes=[pltpu.VMEM((POOL_PAD + POOL_HALO + tm, D_MODEL), F32),
                        pltpu.VMEM((POOL_PAD + POOL_HALO + tm, POOL_GROUP), F32),
                        pltpu.VMEM((POOL_PAD + POOL_HALO + tm, POOL_GROUP), F32)]
        + _pipeline_scratch(tm),
        compiler_params=_params(1),
        name="pool_ffn",
    )(x, x, g, pw, ps, gf, wg, wu, wdn, gfin)


def _wo_ffn(x, o_attn, wo, gf, wg, wu, wdn, gfin, layer, final_norm):
    n_tok = x.shape[0]
    tm = TOKEN_TILE
    n_tiles = n_tok // tm
    row_in = pl.BlockSpec((tm, D_MODEL), lambda i: (jnp.minimum(i, n_tiles - 1), 0))
    row_out = pl.BlockSpec((tm, D_MODEL), lambda i: (jnp.maximum(i - 1, 0), 0))
    return pl.pallas_call(
        functools.partial(_wo_ffn_kernel, layer=layer, final_norm=final_norm),
        grid=(n_tiles + 1,),
        in_specs=[row_in, row_in, _const_spec((D_MODEL, D_MODEL))] + _ffn_specs(),
        out_specs=row_out,
        out_shape=jax.ShapeDtypeStruct((n_tok, D_MODEL), F32),
        scratch_shapes=_pipeline_scratch(tm),
        compiler_params=_params(1),
        name="wo_ffn",
    )(x, o_attn, wo, gf, wg, wu, wdn, gfin)


def kernel(x, positions, norm_mix, norm_ffn, norm_final, pool_w, pool_scale, mla_w_down,
           mla_q_norm, mla_w_uq, mla_kv_norm, mla_w_ukv, mla_w_o, ffn_w_gate, ffn_w_up,
           ffn_w_down):
    batch, seq, d = x.shape
    n_tok = batch * seq
    xt = x.reshape(n_tok, d)
    cos, sin = _rope_tables(positions.reshape(n_tok))
    gfin = norm_final.reshape(1, d)
    n_mixers = 2
    wg, wu, wdn = ffn_w_gate, ffn_w_up, ffn_w_down
    for i in range(DEPTH):
        j = i // n_mixers
        final = i == DEPTH - 1
        gm = norm_mix[i].reshape(1, d)
        gf = norm_ffn[i].reshape(1, d)
        if i % n_mixers == 0:
            xt = _pool_ffn(xt, seq, gm, pool_w[j].astype(BF16), pool_scale[j].reshape(1, d),
                           gf, wg, wu, wdn, gfin, i, final)
        else:
            rope_pad = LANES - QK_ROPE_DIM
            wd = jnp.pad(mla_w_down[j], ((0, 0), (0, rope_pad))).astype(BF16)
            wq = mla_w_uq[j].reshape(Q_LORA_RANK, N_HEADS, QK_HEAD_DIM)
            wqn = wq[:, :, :QK_NOPE_DIM].reshape(Q_LORA_RANK, -1).astype(BF16)
            wqr = jnp.pad(wq[:, :, QK_NOPE_DIM:], ((0, 0), (0, 0), (0, rope_pad)))
            wqr = wqr.reshape(Q_LORA_RANK, -1).astype(BF16)
            wkv = mla_w_ukv[j].reshape(KV_LORA_RANK, N_HEADS, QK_NOPE_DIM + V_HEAD_DIM)
            wkn = wkv[:, :, :QK_NOPE_DIM].reshape(KV_LORA_RANK, -1).astype(BF16)
            wv = wkv[:, :, QK_NOPE_DIM:].reshape(KV_LORA_RANK, -1).astype(BF16)
            qn, qr, kn, kr, v = _mla_proj(
                xt, gm, wd, mla_q_norm[j].reshape(1, -1), mla_kv_norm[j].reshape(1, -1),
                wqn, wqr, wkn, wv, cos, sin)
            o_attn = _attention(qn, qr, kn, kr, v, batch, seq)
            xt = _wo_ffn(xt, o_attn, mla_w_o[j].astype(BF16), gf, wg, wu, wdn, gfin, i, final)
    return xt.reshape(batch, seq, d)
```

```python
import functools

import numpy as np
import jax
import jax.numpy as jnp
from jax import lax
from jax.experimental import pallas as pl
from jax.experimental.pallas import tpu as pltpu

D_MODEL = 1024
DEPTH = 4
POOL_WINDOWS = (2, 4, 8, 16)
POOL_GROUP = D_MODEL // len(POOL_WINDOWS)
POOL_HALO = 16
POOL_PAD = 8
N_HEADS = 8
QK_NOPE_DIM = 128
QK_ROPE_DIM = 64
V_HEAD_DIM = 128
Q_LORA_RANK = 384
KV_LORA_RANK = 128
QK_HEAD_DIM = QK_NOPE_DIM + QK_ROPE_DIM
ROPE_THETA = 10000.0
D_FF = 2816
RMS_EPS = 1e-6

LANES = 128
MXU_DIM = 256
FF_CHUNK = MXU_DIM
DOWN_PAD = Q_LORA_RANK + KV_LORA_RANK + LANES

TOKEN_TILE = 512
ATTN_TILE = 512
PROJ_TILE = 1024
PROJ_SUB_TILE = 512
ROPE_TILE = 1024
GATE_LOAD_ROWS = 128
DOWN_LOAD_ROWS = D_FF // 8
LOAD_SLOTS = 6
VMEM_LIMIT = 56 * 1024 * 1024
BIG_VMEM_LIMIT = 60 * 1024 * 1024

LOG2_E = 1.4426950408889634
BF16 = jnp.bfloat16
F32 = jnp.float32


def _rms(x, g):
    return x * lax.rsqrt(jnp.mean(x * x, axis=-1, keepdims=True) + RMS_EPS) * g


def _dot(a, b):
    return jnp.dot(a, b, preferred_element_type=F32)


def _const_spec(shape):
    return pl.BlockSpec(shape, lambda *_: (0,) * len(shape), pipeline_mode=pl.Buffered(1))


def _params(n_axes, vmem_limit=VMEM_LIMIT):
    return pltpu.CompilerParams(
        dimension_semantics=("arbitrary",) * n_axes,
        vmem_limit_bytes=vmem_limit)


def _rope_table_kernel(pos_ref, inv_ref, cos_ref, sin_ref):
    ang = pos_ref[...].astype(F32) * inv_ref[...]
    cos_ref[...] = jnp.cos(ang)
    sin_ref[...] = jnp.sin(ang)


def _rope_tables(positions):
    n_tok = positions.shape[0]
    half = QK_ROPE_DIM // 2
    tok_per_row = LANES // half
    n_rows = n_tok // tok_per_row
    inv = 1.0 / (ROPE_THETA ** (np.arange(0, QK_ROPE_DIM, 2, dtype=np.float32) / QK_ROPE_DIM))
    inv_row = np.tile(inv, tok_per_row)[None, :]
    pos_rows = jnp.repeat(positions.reshape(n_rows, tok_per_row), half, axis=1)
    spec = pl.BlockSpec((ROPE_TILE, LANES), lambda i: (i, 0))
    cos, sin = pl.pallas_call(
        _rope_table_kernel,
        grid=(n_rows // ROPE_TILE,),
        in_specs=[spec, _const_spec((1, LANES))],
        out_specs=[spec] * 2,
        out_shape=[jax.ShapeDtypeStruct((n_rows, LANES), F32)] * 2,
        compiler_params=_params(1),
        name="rope_tables",
    )(pos_rows, jnp.asarray(inv_row))
    return cos.reshape(n_tok, half), sin.reshape(n_tok, half)


def _rope(x, cos, sin):
    lane = lax.broadcasted_iota(jnp.int32, x.shape, 1)
    half = QK_ROPE_DIM // 2
    swapped = jnp.where(lane < half,
                        pltpu.roll(x, LANES - half, axis=1),
                        pltpu.roll(x, half, axis=1))
    return x * cos + swapped * sin


def _mla_proj_kernel(x_ref, g_ref, wd_ref, qn_g_ref, kvn_g_ref, wqn_ref, wqr_ref,
                     wkn_ref, wv_ref, cos_ref, sin_ref,
                     qn_ref, qr_ref, kn_ref, kr_ref, v_ref):
    tm = x_ref.shape[0]
    subs = [pl.ds(r, PROJ_SUB_TILE) for r in range(0, tm, PROJ_SUB_TILE)]
    sm_scale = QK_HEAD_DIM ** -0.5 * LOG2_E
    pad = jnp.zeros((PROJ_SUB_TILE, LANES - QK_ROPE_DIM), F32)
    h = [_rms(x_ref[r, :], g_ref[...]).astype(BF16) for r in subs]
    d = [_dot(hh, wd_ref[...]) for hh in h]
    cq = [_rms(dd[:, :Q_LORA_RANK], qn_g_ref[...]).astype(BF16) for dd in d]
    ckv = [_rms(dd[:, Q_LORA_RANK:Q_LORA_RANK + KV_LORA_RANK], kvn_g_ref[...]).astype(BF16)
           for dd in d]
    cos = [jnp.concatenate([cos_ref[r, :], cos_ref[r, :], pad], axis=1) for r in subs]
    sin = [jnp.concatenate([-sin_ref[r, :], sin_ref[r, :], pad], axis=1) for r in subs]
    for i, r in enumerate(subs):
        kr_ref[r, :] = _rope(d[i][:, Q_LORA_RANK + KV_LORA_RANK:], cos[i], sin[i]).astype(BF16)
    for i, r in enumerate(subs):
        qn_ref[r, :] = (_dot(cq[i], wqn_ref[...]) * sm_scale).astype(BF16)
    qr = [_dot(c, wqr_ref[...]) for c in cq]
    for i, r in enumerate(subs):
        for hd in range(N_HEADS):
            sl = slice(hd * LANES, (hd + 1) * LANES)
            qr_ref[r, sl] = (_rope(qr[i][:, sl], cos[i], sin[i]) * sm_scale).astype(BF16)
    for i, r in enumerate(subs):
        kn_ref[r, :] = _dot(ckv[i], wkn_ref[...]).astype(BF16)
    for i, r in enumerate(subs):
        v_ref[r, :] = _dot(ckv[i], wv_ref[...]).astype(BF16)


def _mla_proj(x, g, wd, qn_g, kvn_g, wqn, wqr, wkn, wv, cos, sin):
    n_tok = x.shape[0]
    tm = PROJ_TILE
    hv = N_HEADS * LANES
    row = lambda w: pl.BlockSpec((tm, w), lambda i: (i, 0))
    return pl.pallas_call(
        _mla_proj_kernel,
        grid=(n_tok // tm,),
        in_specs=[row(D_MODEL), _const_spec((1, D_MODEL)), _const_spec((D_MODEL, DOWN_PAD)),
                  _const_spec((1, Q_LORA_RANK)), _const_spec((1, KV_LORA_RANK)),
                  _const_spec((Q_LORA_RANK, hv)), _const_spec((Q_LORA_RANK, hv)),
                  _const_spec((KV_LORA_RANK, hv)), _const_spec((KV_LORA_RANK, hv)),
                  row(QK_ROPE_DIM // 2), row(QK_ROPE_DIM // 2)],
        out_specs=[row(hv), row(hv), row(hv), row(LANES), row(hv)],
        out_shape=[jax.ShapeDtypeStruct((n_tok, hv), BF16),
                   jax.ShapeDtypeStruct((n_tok, hv), BF16),
                   jax.ShapeDtypeStruct((n_tok, hv), BF16),
                   jax.ShapeDtypeStruct((n_tok, LANES), BF16),
                   jax.ShapeDtypeStruct((n_tok, hv), BF16)],
        compiler_params=_params(1),
        name="mla_proj",
    )(x, g, wd, qn_g, kvn_g, wqn, wqr, wkn, wv, cos, sin)


def _attn_kernel(qn_ref, qr_ref, kn_ref, kr_ref, v_ref, o_ref, m_ref, acc_ref, s_ref):
    qi = pl.program_id(1)
    t = ATTN_TILE
    ones = jnp.ones((2 * t, LANES), BF16)

    def key_rows(start, n):
        return pl.ds(pl.multiple_of(start, t), n)

    def scores(hd, start, n):
        sl = slice(hd * LANES, (hd + 1) * LANES)
        rows = key_rows(start, n)
        q = jnp.concatenate([qn_ref[:, sl], qr_ref[:, sl]], axis=1)
        k = jnp.concatenate([kn_ref[rows, sl], kr_ref[rows, :]], axis=1)
        return lax.dot_general(q, k, (((1,), (1,)), ((), ())),
                               preferred_element_type=F32)

    def update(hd, start, s, diag_col=None, first=False):
        sl = slice(hd * LANES, (hd + 1) * LANES)
        n = s.shape[1]
        if diag_col is not None:
            row = lax.broadcasted_iota(jnp.int32, s.shape, 0)
            col = lax.broadcasted_iota(jnp.int32, s.shape, 1)
            s = jnp.where(col - diag_col <= row, s, -jnp.inf)
        m_new = jnp.broadcast_to(jnp.max(s, axis=1, keepdims=True), (t, LANES))
        if not first:
            m_prev = m_ref[hd]
            m_new = jnp.maximum(m_prev, m_new)
            alpha = jnp.exp2(m_prev - m_new)
        p = jnp.exp2(s - jnp.concatenate([m_new] * (n // LANES), axis=1))
        v_ext = jnp.concatenate([v_ref[key_rows(start, n), sl], ones[:n]], axis=1)
        pv = _dot(p.astype(BF16), v_ext)
        if not first:
            pv = jnp.concatenate([alpha, alpha], axis=1) * acc_ref[hd] + pv
        acc_ref[hd] = pv
        m_ref[hd] = m_new

    def sweep(start, n, diag_col, carry_next, first=False):
        s = s_ref[...]
        if n > t:
            s = jnp.concatenate([s, scores(0, start + t, n - t)], axis=1)
        for hd in range(N_HEADS):
            s_next = None
            if hd + 1 < N_HEADS:
                s_next = scores(hd + 1, start, n)
            elif carry_next:
                s_next = scores(0, start + n, t)
            update(hd, start, s, diag_col, first)
            s = s_next
        if carry_next:
            s_ref[...] = s

    def body(kp, carry):
        sweep(kp * 2 * t, 2 * t, None, True)
        return carry

    s_ref[...] = scores(0, 0, t)
    n_pairs = qi // 2

    @pl.when(n_pairs == 0)
    def _():
        m_ref[...] = jnp.full(m_ref.shape, -jnp.inf, F32)
        acc_ref[...] = jnp.zeros(acc_ref.shape, F32)

    @pl.when(n_pairs > 0)
    def _():
        sweep(0, 2 * t, None, True, first=True)

    lax.fori_loop(1, n_pairs, body, 0)

    @pl.when(qi % 2 == 1)
    def _():
        sweep(n_pairs * 2 * t, 2 * t, t, False)

    @pl.when(qi % 2 == 0)
    def _():
        sweep(qi * t, t, 0, False)

    for hd in range(N_HEADS):
        acc = acc_ref[hd]
        o_ref[:, hd * LANES:(hd + 1) * LANES] = (acc[:, :LANES] / acc[:, LANES:]).astype(BF16)


def _attention(qn, qr, kn, kr, v, batch, seq):
    t = ATTN_TILE
    nq = seq // t
    hv = N_HEADS * LANES
    q_spec = pl.BlockSpec((t, hv), lambda b, i: (b * nq + i, 0))
    seq_spec = lambda w: pl.BlockSpec((seq, w), lambda b, i: (b, 0))
    return pl.pallas_call(
        _attn_kernel,
        grid=(batch, nq),
        in_specs=[q_spec, q_spec, seq_spec(hv), seq_spec(LANES), seq_spec(hv)],
        out_specs=q_spec,
        out_shape=jax.ShapeDtypeStruct((batch * seq, hv), BF16),
        scratch_shapes=[pltpu.VMEM((N_HEADS, t, LANES), F32),
                        pltpu.VMEM((N_HEADS, t, 2 * LANES), F32),
                        pltpu.VMEM((t, t), F32)],
        compiler_params=_params(2, BIG_VMEM_LIMIT),
        name="mla_attention",
    )(qn, qr, kn, kr, v)


def _ffn_stage(x1_ref, h_ref, wg_ref, wu_ref, wdn_ref, gfin_ref, a_ref, o_ref, final_norm,
               side_work):
    h = h_ref[...]
    side_work = list(side_work)
    for c in range(D_FF // FF_CHUNK):
        sl = slice(c * FF_CHUNK, (c + 1) * FF_CHUNK)
        gate = _dot(h, wg_ref[:, sl])
        up = _dot(h, wu_ref[:, sl])
        a_ref[:, sl] = (gate / (1.0 + jnp.exp(-gate)) * up).astype(BF16)
        if side_work:
            side_work.pop(0)()
    for piece in side_work:
        piece()
    out = x1_ref[...] + _dot(a_ref[...], wdn_ref[...])
    if final_norm:
        out = _rms(out, gfin_ref[...])
    o_ref[...] = out


def _load_ffn_weights(layer, hbm_refs, vmem_refs, stage_gate, stage_down, sem):
    jobs = []
    for src, dst, stage, rows in zip(hbm_refs, vmem_refs,
                                     (stage_gate, stage_gate, stage_down),
                                     (GATE_LOAD_ROWS, GATE_LOAD_ROWS, DOWN_LOAD_ROWS)):
        for r in range(0, dst.shape[0], rows):
            jobs.append((src.at[layer, pl.ds(r, rows), :], dst.at[pl.ds(r, rows), :], stage))

    def copy(j):
        src, _, stage = jobs[j]
        return pltpu.make_async_copy(src, stage.at[j % LOAD_SLOTS], sem.at[j % LOAD_SLOTS])

    for j in range(LOAD_SLOTS - 1):
        copy(j).start()
    for j, (_, dst, stage) in enumerate(jobs):
        if j + LOAD_SLOTS - 1 < len(jobs):
            copy(j + LOAD_SLOTS - 1).start()
        copy(j).wait()
        dst[...] = stage[j % LOAD_SLOTS].astype(BF16)


def _pipelined_step(mixer_pieces, gf_ref, ffn_refs, o_ref, x1_slots, h_slots, final_norm,
                    load_weights):
    i = pl.program_id(0)

    def mixer_and_norm(cur):
        def ffn_norm():
            h_slots[cur][...] = _rms(x1_slots[cur][...], gf_ref[...]).astype(BF16)

        return mixer_pieces(x1_slots[cur]) + [ffn_norm]

    @pl.when(i == 0)
    def _():
        load_weights()
        for piece in mixer_and_norm(0):
            piece()

    for cur in (0, 1):
        @pl.when((i > 0) & (i % 2 == cur))
        def _():
            _ffn_stage(x1_slots[1 - cur], h_slots[1 - cur], *ffn_refs, o_ref, final_norm,
                       mixer_and_norm(cur))


def _pool_ffn_kernel(x_ref, halo_ref, g_ref, pw_ref, ps_ref, gf_ref, wg_hbm, wu_hbm, wdn_hbm,
                     gfin_ref, o_ref, ext_ref, st_a, st_b, a_ref, x1_a, x1_b, h_a, h_b,
                     wg_ref, wu_ref, wdn_ref, stage_gate, stage_down, sem,
                     *, n_tiles, tiles_per_seq, layer, final_norm):
    tm = x_ref.shape[0]
    blk = jnp.minimum(pl.program_id(0), n_tiles - 1) % tiles_per_seq
    top = POOL_PAD + POOL_HALO
    n_ext = POOL_HALO + tm

    def mixer_pieces(x1_ref):
        def norm():
            for ref in (ext_ref, st_a, st_b):
                ref[:POOL_PAD, :] = jnp.zeros((POOL_PAD, ref.shape[1]), F32)
            ext_ref[top:, :] = _rms(x_ref[...], g_ref[...])
            halo = _rms(halo_ref[...], g_ref[...])
            ext_ref[POOL_PAD:top, :] = jnp.where(blk == 0, 0.0, halo)

        def group(g, win):
            sl = slice(g * POOL_GROUP, (g + 1) * POOL_GROUP)
            acc = ext_ref[POOL_PAD:, sl] + ext_ref[POOL_PAD - 1:POOL_PAD - 1 + n_ext, sl]
            width, stage = 2, 0
            while width < win:
                buf = (st_a, st_b)[stage % 2]
                buf[POOL_PAD:, :] = acc
                acc = buf[POOL_PAD:, :] + buf[POOL_PAD - width:POOL_PAD - width + n_ext, :]
                width, stage = 2 * width, stage + 1
            pos = blk * tm + lax.broadcasted_iota(jnp.int32, (tm, 1), 0)
            avail = (pos + 1).astype(F32)
            p = acc[POOL_HALO:] / jnp.minimum(avail, float(win)) - ext_ref[top:, sl]
            y = _dot(p.astype(BF16), pw_ref[g])
            x1_ref[:, sl] = x_ref[:, sl] + y * ps_ref[:, sl]

        return [norm] + [functools.partial(group, g, win)
                         for g, win in enumerate(POOL_WINDOWS)]

    load_weights = functools.partial(
        _load_ffn_weights, layer, (wg_hbm, wu_hbm, wdn_hbm), (wg_ref, wu_ref, wdn_ref),
        stage_gate, stage_down, sem)
    _pipelined_step(mixer_pieces, gf_ref, (wg_ref, wu_ref, wdn_ref, gfin_ref, a_ref), o_ref,
                    (x1_a, x1_b), (h_a, h_b), final_norm, load_weights)


def _wo_ffn_kernel(x_ref, o_attn_ref, wo_ref, gf_ref, wg_hbm, wu_hbm, wdn_hbm,
                   gfin_ref, o_ref, a_ref, x1_a, x1_b, h_a, h_b,
                   wg_ref, wu_ref, wdn_ref, stage_gate, stage_down, sem, *, layer, final_norm):
    def mixer_pieces(x1_ref):
        def out_proj():
            x1_ref[...] = x_ref[...] + _dot(o_attn_ref[...], wo_ref[...])

        return [out_proj]

    load_weights = functools.partial(
        _load_ffn_weights, layer, (wg_hbm, wu_hbm, wdn_hbm), (wg_ref, wu_ref, wdn_ref),
        stage_gate, stage_down, sem)
    _pipelined_step(mixer_pieces, gf_ref, (wg_ref, wu_ref, wdn_ref, gfin_ref, a_ref), o_ref,
                    (x1_a, x1_b), (h_a, h_b), final_norm, load_weights)


def _ffn_specs():
    hbm = pl.BlockSpec(memory_space=pl.ANY)
    return [_const_spec((1, D_MODEL)), hbm, hbm, hbm, _const_spec((1, D_MODEL))]


def _pipeline_scratch(tm):
    return [pltpu.VMEM((tm, D_FF), BF16),
            pltpu.VMEM((tm, D_MODEL), F32), pltpu.VMEM((tm, D_MODEL), F32),
            pltpu.VMEM((tm, D_MODEL), BF16), pltpu.VMEM((tm, D_MODEL), BF16),
            pltpu.VMEM((D_MODEL, D_FF), BF16), pltpu.VMEM((D_MODEL, D_FF), BF16),
            pltpu.VMEM((D_FF, D_MODEL), BF16),
            pltpu.VMEM((LOAD_SLOTS, GATE_LOAD_ROWS, D_FF), F32),
            pltpu.VMEM((LOAD_SLOTS, DOWN_LOAD_ROWS, D_MODEL), F32),
            pltpu.SemaphoreType.DMA((LOAD_SLOTS,))]


def _pool_ffn(x, seq, g, pw, ps, gf, wg, wu, wdn, gfin, layer, final_norm):
    n_tok = x.shape[0]
    tm = TOKEN_TILE
    n_tiles = n_tok // tm
    halo_per_tile = tm // POOL_HALO
    tile_in = lambda i: jnp.minimum(i, n_tiles - 1)
    row_in = pl.BlockSpec((tm, D_MODEL), lambda i: (tile_in(i), 0))
    halo = pl.BlockSpec((POOL_HALO, D_MODEL),
                        lambda i: (jnp.maximum(tile_in(i) * halo_per_tile - 1, 0), 0))
    row_out = pl.BlockSpec((tm, D_MODEL), lambda i: (jnp.maximum(i - 1, 0), 0))
    return pl.pallas_call(
        functools.partial(_pool_ffn_kernel, n_tiles=n_tiles, tiles_per_seq=seq // tm,
                          layer=layer, final_norm=final_norm),
        grid=(n_tiles + 1,),
        in_specs=[row_in, halo, _const_spec((1, D_MODEL)),
                  _const_spec((len(POOL_WINDOWS), POOL_GROUP, POOL_GROUP)),
                  _const_spec((1, D_MODEL))] + _ffn_specs(),
        out_specs=row_out,
        out_shape=jax.ShapeDtypeStruct((n_tok, D_MODEL), F32),
        scratch_shapes=[pltpu.VMEM((POOL_PAD + POOL_HALO + tm, D_MODEL), F32),
                        pltpu.VMEM((POOL_PAD + POOL_HALO + tm, POOL_GROUP), F32),
                        pltpu.VMEM((POOL_PAD + POOL_HALO + tm, POOL_GROUP), F32)]
        + _pipeline_scratch(tm),
        compiler_params=_params(1, BIG_VMEM_LIMIT),
        name="pool_ffn",
    )(x, x, g, pw, ps, gf, wg, wu, wdn, gfin)


def _wo_ffn(x, o_attn, wo, gf, wg, wu, wdn, gfin, layer, final_norm):
    n_tok = x.shape[0]
    tm = TOKEN_TILE
    n_tiles = n_tok // tm
    row_in = pl.BlockSpec((tm, D_MODEL), lambda i: (jnp.minimum(i, n_tiles - 1), 0))
    row_out = pl.BlockSpec((tm, D_MODEL), lambda i: (jnp.maximum(i - 1, 0), 0))
    return pl.pallas_call(
        functools.partial(_wo_ffn_kernel, layer=layer, final_norm=final_norm),
        grid=(n_tiles + 1,),
        in_specs=[row_in, row_in, _const_spec((D_MODEL, D_MODEL))] + _ffn_specs(),
        out_specs=row_out,
        out_shape=jax.ShapeDtypeStruct((n_tok, D_MODEL), F32),
        scratch_shapes=_pipeline_scratch(tm),
        compiler_params=_params(1, BIG_VMEM_LIMIT),
        name="wo_ffn",
    )(x, o_attn, wo, gf, wg, wu, wdn, gfin)


def kernel(x, positions, norm_mix, norm_ffn, norm_final, pool_w, pool_scale, mla_w_down,
           mla_q_norm, mla_w_uq, mla_kv_norm, mla_w_ukv, mla_w_o, ffn_w_gate, ffn_w_up,
           ffn_w_down):
    batch, seq, d = x.shape
    n_tok = batch * seq
    xt = x.reshape(n_tok, d)
    cos, sin = _rope_tables(positions.reshape(n_tok))
    gfin = norm_final.reshape(1, d)
    n_mixers = 2
    wg, wu, wdn = ffn_w_gate, ffn_w_up, ffn_w_down
    for i in range(DEPTH):
        j = i // n_mixers
        final = i == DEPTH - 1
        gm = norm_mix[i].reshape(1, d)
        gf = norm_ffn[i].reshape(1, d)
        if i % n_mixers == 0:
            xt = _pool_ffn(xt, seq, gm, pool_w[j].astype(BF16), pool_scale[j].reshape(1, d),
                           gf, wg, wu, wdn, gfin, i, final)
        else:
            rope_pad = LANES - QK_ROPE_DIM
            wd = jnp.pad(mla_w_down[j], ((0, 0), (0, rope_pad))).astype(BF16)
            wq = mla_w_uq[j].reshape(Q_LORA_RANK, N_HEADS, QK_HEAD_DIM)
            wqn = wq[:, :, :QK_NOPE_DIM].reshape(Q_LORA_RANK, -1).astype(BF16)
            wqr = jnp.pad(wq[:, :, QK_NOPE_DIM:], ((0, 0), (0, 0), (0, rope_pad)))
            wqr = wqr.reshape(Q_LORA_RANK, -1).astype(BF16)
            wkv = mla_w_ukv[j].reshape(KV_LORA_RANK, N_HEADS, QK_NOPE_DIM + V_HEAD_DIM)
            wkn = wkv[:, :, :QK_NOPE_DIM].reshape(KV_LORA_RANK, -1).astype(BF16)
            wv = wkv[:, :, QK_NOPE_DIM:].reshape(KV_LORA_RANK, -1).astype(BF16)
            qn, qr, kn, kr, v = _mla_proj(
                xt, gm, wd, mla_q_norm[j].reshape(1, -1), mla_kv_norm[j].reshape(1, -1),
                wqn, wqr, wkn, wv, cos, sin)
            o_attn = _attention(qn, qr, kn, kr, v, batch, seq)
            xt = _wo_ffn(xt, o_attn, mla_w_o[j].astype(BF16), gf, wg, wu, wdn, gfin, i, final)
    return xt.reshape(batch, seq, d)
```
